```python
import jax, jax.numpy as jnp
from jax import lax
import numpy as np

D_MODEL = 1024
BATCH = 8
SEQ = 2048
DEPTH = 1

CHUNK = 64
Q_BLOCK = 128
SB_HEADS = 8
SB_HEAD_DIM = 64
SB_WIDTH = SB_HEADS * SB_HEAD_DIM
RET_HEADS = 4
RET_KEY_DIM = 128
RET_VALUE_DIM = 128
RET_QK_WIDTH = RET_HEADS * RET_KEY_DIM
RET_V_WIDTH = RET_HEADS * RET_VALUE_DIM
D_FF = 4 * D_MODEL
RMS_EPS = 1e-6
GN_EPS = 1e-5
ROPE_BASE = 10000.0

SPLIT_SIZES = (SB_WIDTH, SB_WIDTH, SB_WIDTH,
               RET_QK_WIDTH, RET_QK_WIDTH, RET_V_WIDTH, RET_V_WIDTH,
               D_MODEL, D_MODEL)
SPLIT_POINTS = tuple(int(p) for p in np.cumsum(SPLIT_SIZES)[:-1])
W_IN_COLS = int(sum(SPLIT_SIZES))

kernel_name = "sandwich_macaron_stickbreak_retention_hybrid"


def rms_norm(x, g):
    xf = x.astype(jnp.float32)
    y = xf * lax.rsqrt(jnp.mean(xf * xf, axis=-1, keepdims=True) + RMS_EPS)
    return (y * g.astype(jnp.float32)).astype(x.dtype)


def swiglu(h, w_gate, w_up, w_down):
    return (jax.nn.silu(h @ w_gate) * (h @ w_up)) @ w_down


def rotary(x, pos):
    half = x.shape[-1] // 2
    inv_freq = ROPE_BASE ** (-jnp.arange(half, dtype=jnp.float32) / half)
    ang = pos[:, None] * inv_freq[None, :]
    cos = jnp.cos(ang)[None, :, None, :]
    sin = jnp.sin(ang)[None, :, None, :]
    x1, x2 = x[..., :half], x[..., half:]
    return jnp.concatenate([x1 * cos - x2 * sin, x1 * sin + x2 * cos], axis=-1)


def stick_breaking_attention(q, k, v):
    seq = q.shape[1]
    scale = q.shape[-1] ** -0.5
    outs = []
    for i in range(seq // Q_BLOCK):
        start, end = i * Q_BLOCK, (i + 1) * Q_BLOCK
        qb = q[:, start:end]
        kb = k[:, :end]
        vb = v[:, :end]
        z = jnp.einsum('bqhd,bkhd->bhqk', qb, kb) * scale
        t_pos = start + jnp.arange(Q_BLOCK)
        s_pos = jnp.arange(end)
        causal = s_pos[None, :] < t_pos[:, None]
        log_keep = jnp.where(causal, jax.nn.log_sigmoid(-z), 0.0)
        suffix = lax.cumsum(log_keep, axis=3, reverse=True)
        between = jnp.concatenate(
            [suffix[..., 1:], jnp.zeros_like(suffix[..., :1])], axis=-1)
        w = jnp.where(causal, jnp.exp(jax.nn.log_sigmoid(z) + between), 0.0)
        outs.append(jnp.einsum('bhqk,bkhd->bqhd', w, vb))
    return jnp.concatenate(outs, axis=1)


def retention_chunkwise(q, k, v, log_gamma):
    b, seq, h, dk = q.shape
    dv = v.shape[-1]
    n = seq // CHUNK
    qc = q.reshape(b, n, CHUNK, h, dk)
    kc = k.reshape(b, n, CHUNK, h, dk)
    vc = v.reshape(b, n, CHUNK, h, dv)
    idx = jnp.arange(CHUNK, dtype=jnp.float32)
    diff = idx[:, None] - idx[None, :]
    inner_decay = jnp.where(diff[None] >= 0,
                            jnp.exp(jnp.maximum(diff, 0.0)[None] * log_gamma[:, None, None]),
                            0.0)
    scores = jnp.einsum('bnchd,bnmhd->bnhcm', qc, kc) * inner_decay
    inner = jnp.einsum('bnhcm,bnmhe->bnche', scores, vc)
    k_decay = jnp.exp((CHUNK - 1 - idx)[:, None] * log_gamma[None, :])
    chunk_kv = jnp.einsum('bnmhd,bnmhe->nbhde', kc * k_decay[None, None, :, :, None], vc)
    chunk_decay = jnp.exp(CHUNK * log_gamma)[None, :, None, None]

    def step(state, kv):
        return chunk_decay * state + kv, state

    _, prev_states = lax.scan(step, jnp.zeros((b, h, dk, dv), jnp.float32), chunk_kv)
    q_decay = jnp.exp((idx + 1.0)[:, None] * log_gamma[None, :])
    cross = jnp.einsum('bnchd,nbhde->bnche', qc * q_decay[None, None, :, :, None], prev_states)
    return (inner + cross).reshape(b, seq, h, dv)


def head_group_norm(o):
    mu = jnp.mean(o, axis=-1, keepdims=True)
    var = jnp.mean(jnp.square(o - mu), axis=-1, keepdims=True)
    return (o - mu) * lax.rsqrt(var + GN_EPS)


def setup_inputs(seed: int = 0) -> dict:
    key = jax.random.key(seed)
    ks = jax.random.split(key, 20)

    def dense(k, fan_in, fan_out):
        return jax.random.normal(k, (DEPTH, fan_in, fan_out), jnp.float32) * fan_in ** -0.5

    def gain(k):
        return 1.0 + 0.02 * jax.random.normal(k, (DEPTH, D_MODEL), jnp.float32)

    return {
        "x": jax.random.normal(ks[0], (BATCH, SEQ, D_MODEL), jnp.float32),
        "g_ffn1_pre": gain(ks[1]),
        "g_ffn1_post": gain(ks[2]),
        "w_ffn1_gate": dense(ks[3], D_MODEL, D_FF),
        "w_ffn1_up": dense(ks[4], D_MODEL, D_FF),
        "w_ffn1_down": dense(ks[5], D_FF, D_MODEL),
        "g_mix_pre": gain(ks[6]),
        "w_in": dense(ks[7], D_MODEL, W_IN_COLS),
        "w_proj_sb": dense(ks[8], SB_WIDTH, D_MODEL),
        "w_proj_ret": dense(ks[9], RET_V_WIDTH, D_MODEL),
        "w_out": dense(ks[10], D_MODEL, D_MODEL),
        "g_mix_post": gain(ks[11]),
        "g_ffn2_pre": gain(ks[12]),
        "g_ffn2_post": gain(ks[13]),
        "w_ffn2_gate": dense(ks[14], D_MODEL, D_FF),
        "w_ffn2_up": dense(ks[15], D_MODEL, D_FF),
        "w_ffn2_down": dense(ks[16], D_FF, D_MODEL),
    }


def reference(x, g_ffn1_pre, g_ffn1_post, w_ffn1_gate, w_ffn1_up, w_ffn1_down,
              g_mix_pre, w_in, w_proj_sb, w_proj_ret, w_out, g_mix_post,
              g_ffn2_pre, g_ffn2_post, w_ffn2_gate, w_ffn2_up, w_ffn2_down):
    b, seq, _ = x.shape
    f32 = jnp.float32
    pos = jnp.arange(seq, dtype=f32)
    log_gamma = jnp.log(1.0 - 2.0 ** (-5.0 - jnp.arange(RET_HEADS, dtype=f32)))

    for l in range(DEPTH):
        h = rms_norm(x, g_ffn1_pre[l])
        x = x + 0.5 * rms_norm(swiglu(h, w_ffn1_gate[l], w_ffn1_up[l], w_ffn1_down[l]),
                               g_ffn1_post[l])

        h = rms_norm(x, g_mix_pre[l])
        proj = h @ w_in[l]
        (q_sb, k_sb, v_sb, q_r, k_r, v_r, g_r,
         gate_sb, gate_ret) = jnp.split(proj, SPLIT_POINTS, axis=-1)

        shp_sb = (b, seq, SB_HEADS, SB_HEAD_DIM)
        o_sb = stick_breaking_attention(q_sb.reshape(shp_sb).astype(f32),
                                        k_sb.reshape(shp_sb).astype(f32),
                                        v_sb.reshape(shp_sb).astype(f32))
        o_sb = o_sb.reshape(b, seq, SB_WIDTH).astype(x.dtype)

        shp_qk = (b, seq, RET_HEADS, RET_KEY_DIM)
        qr = rotary(q_r.reshape(shp_qk).astype(f32), pos)
        kr = rotary(k_r.reshape(shp_qk).astype(f32), pos) * RET_KEY_DIM ** -0.5
        vr = v_r.reshape(b, seq, RET_HEADS, RET_VALUE_DIM).astype(f32)
        o_r = head_group_norm(retention_chunkwise(qr, kr, vr, log_gamma))
        o_r = (o_r.reshape(b, seq, RET_V_WIDTH) * jax.nn.silu(g_r.astype(f32))).astype(x.dtype)

        merged = (jax.nn.sigmoid(gate_sb) * (o_sb @ w_proj_sb[l])
                  + jax.nn.sigmoid(gate_ret) * (o_r @ w_proj_ret[l]))
        x = x + rms_norm(merged @ w_out[l], g_mix_post[l])

        h = rms_norm(x, g_ffn2_pre[l])
        x = x + 0.5 * rms_norm(swiglu(h, w_ffn2_gate[l], w_ffn2_up[l], w_ffn2_down[l]),
                               g_ffn2_post[l])
    return x
```

```python
import functools

import numpy as np
import jax
import jax.numpy as jnp
from jax import lax
from jax.experimental import pallas as pl
from jax.experimental.pallas import tpu as pltpu

F32 = jnp.float32
BF16 = jnp.bfloat16

D_MODEL = 1024
D_FF = 4 * D_MODEL
SB_HEADS = 8
SB_HEAD_DIM = 64
SB_WIDTH = SB_HEADS * SB_HEAD_DIM
RET_HEADS = 4
RET_KEY_DIM = 128
RET_WIDTH = RET_HEADS * RET_KEY_DIM
RMS_EPS = 1e-6
GN_EPS = 1e-5
ROPE_BASE = 10000.0

LANES = 128
VMEM_LIMIT = 56 * 1024 * 1024

FFN_TM = 512
FFN_TF = 1024
MIX_TM = 512
SB_TQ = 256
SB_TK = 128
RET_CHUNK = 256


def _rms(x, g):
    ms = jnp.mean(x * x, axis=-1, keepdims=True)
    return x * lax.rsqrt(ms + RMS_EPS) * g


def _resident(shape):
    nd = len(shape)
    return pl.BlockSpec(shape, lambda *_: (0,) * nd, pipeline_mode=pl.Buffered(1))


def _ffn_kernel(x_ref, gpre_ref, gpost_ref, wg_ref, wu_ref, wd_ref, o_ref):
    x = x_ref[...]
    h = _rms(x, gpre_ref[...]).astype(BF16)
    acc = None
    for j in range(D_FF // FFN_TF):
        g = jnp.dot(h, wg_ref[j], preferred_element_type=F32)
        u = jnp.dot(h, wu_ref[j], preferred_element_type=F32)
        a = (g * jax.nn.sigmoid(g) * u).astype(BF16)
        d = jnp.dot(a, wd_ref[j], preferred_element_type=F32)
        acc = d if acc is None else acc + d
    o_ref[...] = x + 0.5 * _rms(acc, gpost_ref[...])


def _ffn(x2d, g_pre, g_post, wg, wu, wd):
    m = x2d.shape[0]
    nj = D_FF // FFN_TF
    tile = pl.BlockSpec((FFN_TM, D_MODEL), lambda i: (i, 0))
    gain = pl.BlockSpec((1, D_MODEL), lambda i: (0, 0))
    return pl.pallas_call(
        _ffn_kernel,
        grid=(m // FFN_TM,),
        in_specs=[tile, gain, gain,
                  _resident((nj, D_MODEL, FFN_TF)),
                  _resident((nj, D_MODEL, FFN_TF)),
                  _resident((nj, FFN_TF, D_MODEL))],
        out_specs=tile,
        out_shape=jax.ShapeDtypeStruct((m, D_MODEL), F32),
        compiler_params=pltpu.CompilerParams(
            dimension_semantics=("arbitrary",), vmem_limit_bytes=VMEM_LIMIT),
        name="ffn",
    )(x2d, g_pre, g_post, wg, wu, wd)


def _mix_in_kernel(x_ref, g_ref, w_ref, cos_ref, sin_ref,
                   qsb_ref, ksb_ref, vsb_ref, qr_ref, kr_ref, vr_ref,
                   gr_ref, gsb_ref, gret_ref):
    h = _rms(x_ref[...], g_ref[...]).astype(BF16)

    def mm(lo, width):
        return jnp.dot(h, w_ref[:, lo:lo + width], preferred_element_type=F32)

    def rotary(t):
        cosf = cos_ref[...]
        sinf = sin_ref[...]
        parts = []
        for hd in range(RET_HEADS):
            th = t[:, hd * LANES:(hd + 1) * LANES]
            parts.append(th * cosf + pltpu.roll(th, LANES // 2, axis=1) * sinf)
        return jnp.concatenate(parts, axis=1)

    c = 0
    qsb_ref[...] = (mm(c, SB_WIDTH) * (SB_HEAD_DIM ** -0.5)).astype(BF16); c += SB_WIDTH
    ksb_ref[...] = mm(c, SB_WIDTH).astype(BF16); c += SB_WIDTH
    vsb_ref[...] = mm(c, SB_WIDTH).astype(BF16); c += SB_WIDTH
    qr_ref[...] = rotary(mm(c, RET_WIDTH)).astype(BF16); c += RET_WIDTH
    kr_ref[...] = (rotary(mm(c, RET_WIDTH)) * (RET_KEY_DIM ** -0.5)).astype(BF16); c += RET_WIDTH
    vr_ref[...] = mm(c, RET_WIDTH).astype(BF16); c += RET_WIDTH
    gr_ref[...] = mm(c, RET_WIDTH); c += RET_WIDTH
    gsb_ref[...] = mm(c, D_MODEL); c += D_MODEL
    gret_ref[...] = mm(c, D_MODEL)


def _mix_in(x2d, g, w_in, cosf, sinf, seq):
    m = x2d.shape[0]
    n_in = w_in.shape[1]
    pos_blocks = seq // MIX_TM

    def tile(width):
        return pl.BlockSpec((MIX_TM, width), lambda i: (i, 0))

    rope = pl.BlockSpec((MIX_TM, LANES), lambda i: (i % pos_blocks, 0))
    widths = [SB_WIDTH] * 3 + [RET_WIDTH] * 3 + [RET_WIDTH, D_MODEL, D_MODEL]
    dtypes = [BF16] * 6 + [F32] * 3
    return pl.pallas_call(
        _mix_in_kernel,
        grid=(m // MIX_TM,),
        in_specs=[tile(D_MODEL), pl.BlockSpec((1, D_MODEL), lambda i: (0, 0)),
                  _resident((D_MODEL, n_in)), rope, rope],
        out_specs=[tile(w) for w in widths],
        out_shape=[jax.ShapeDtypeStruct((m, w), dt) for w, dt in zip(widths, dtypes)],
        compiler_params=pltpu.CompilerParams(
            dimension_semantics=("arbitrary",), vmem_limit_bytes=VMEM_LIMIT),
        name="mix_in",
    )(x2d, g, w_in, cosf, sinf)


def _sb_kernel(q_ref, k_ref, v_ref, tri_ref, o_ref):
    qi = pl.program_id(2)
    q = q_ref[0]
    lane_q = lax.broadcasted_iota(jnp.int32, (SB_TQ, LANES), 1)
    lane_k = lax.broadcasted_iota(jnp.int32, (SB_TK, LANES), 1)
    row = lax.broadcasted_iota(jnp.int32, (SB_TQ, SB_TK), 0)
    col = lax.broadcasted_iota(jnp.int32, (SB_TQ, SB_TK), 1)
    t_pos = qi * SB_TQ + row
    zero_q = jnp.zeros_like(q)
    q_heads = (jnp.where(lane_q < SB_HEAD_DIM, q, zero_q),
               jnp.where(lane_q >= SB_HEAD_DIM, q, zero_q))
    tri = tri_ref[...]
    n_kb = (qi + 1) * (SB_TQ // SB_TK)

    def body(i, carry):
        acc, later = carry[0], carry[1:]
        kb = n_kb - 1 - i
        start = pl.multiple_of(kb * SB_TK, SB_TK)
        k = k_ref[0, pl.ds(start, SB_TK), :]
        v = v_ref[0, pl.ds(start, SB_TK), :]
        zero_v = jnp.zeros_like(v)
        v_heads = (jnp.where(lane_k < SB_HEAD_DIM, v, zero_v),
                   jnp.where(lane_k >= SB_HEAD_DIM, v, zero_v))
        causal = (start + col) < t_pos
        new_later = []
        for hd in range(2):
            z = lax.dot_general(q_heads[hd], k, (((1,), (1,)), ((), ())),
                                preferred_element_type=F32)
            softplus = jnp.maximum(z, 0.0) + jnp.log(1.0 + jnp.exp(-jnp.abs(z)))
            log_keep = jnp.where(causal, -softplus, 0.0)
            hi = log_keep.astype(BF16)
            lo = (log_keep - hi.astype(F32)).astype(BF16)
            sums = jnp.dot(jnp.concatenate([hi, lo], axis=1), tri,
                           preferred_element_type=F32)
            between = sums[:, :SB_TK] + later[hd]
            w = jnp.where(causal, jnp.exp(z - softplus + between), 0.0)
            acc = acc + jnp.dot(w.astype(BF16), v_heads[hd],
                                preferred_element_type=F32)
            new_later.append(later[hd] + sums[:, SB_TK:])
        return (acc, *new_later)

    zeros = jnp.zeros((SB_TQ, LANES), F32)
    acc = lax.fori_loop(0, n_kb, body, (zeros, zeros, zeros))[0]
    o_ref[0] = acc.astype(o_ref.dtype)


def _sb_attention(q, k, v, tri):
    b, seq, _ = q.shape
    pairs = SB_WIDTH // LANES
    qspec = pl.BlockSpec((1, SB_TQ, LANES), lambda bi, p, i: (bi, i, p))
    kvspec = pl.BlockSpec((1, seq, LANES), lambda bi, p, i: (bi, 0, p))
    return pl.pallas_call(
        _sb_kernel,
        grid=(b, pairs, seq // SB_TQ),
        in_specs=[qspec, kvspec, kvspec,
                  pl.BlockSpec((2 * SB_TK, 2 * SB_TK), lambda bi, p, i: (0, 0))],
        out_specs=qspec,
        out_shape=jax.ShapeDtypeStruct((b, seq, SB_WIDTH), BF16),
        compiler_params=pltpu.CompilerParams(
            dimension_semantics=("arbitrary", "arbitrary", "arbitrary"),
            vmem_limit_bytes=VMEM_LIMIT),
        name="sb_attn",
    )(q, k, v, tri)


def _ret_kernel(lg_ref, q_ref, k_ref, v_ref, g_ref, o_ref):
    seq = q_ref.shape[1]
    c = RET_CHUNK
    lg = lg_ref[pl.program_id(1)]
    diff = (lax.broadcasted_iota(jnp.int32, (c, c), 0)
            - lax.broadcasted_iota(jnp.int32, (c, c), 1)).astype(F32)
    inner_decay = jnp.where(diff >= 0, jnp.exp(jnp.maximum(diff, 0.0) * lg), 0.0)
    idx = lax.broadcasted_iota(jnp.int32, (c, 1), 0).astype(F32)
    q_decay = jnp.exp((idx + 1.0) * lg)
    k_decay = jnp.exp((c - 1.0 - idx) * lg)
    chunk_decay = jnp.exp(jnp.full((1, LANES), float(c), F32) * lg)

    def body(n, state):
        sl = pl.ds(pl.multiple_of(n * c, c), c)
        q = q_ref[0, sl, :]
        k = k_ref[0, sl, :]
        v = v_ref[0, sl, :]
        scores = lax.dot_general(q, k, (((1,), (1,)), ((), ())),
                                 preferred_element_type=F32) * inner_decay
        inner = jnp.dot(scores.astype(BF16), v, preferred_element_type=F32)
        cross = jnp.dot((q.astype(F32) * q_decay).astype(BF16), state.astype(BF16),
                        preferred_element_type=F32)
        o = inner + cross
        kd_t = (k.astype(F32) * k_decay).T.astype(BF16)
        state = chunk_decay * state + jnp.dot(kd_t, v, preferred_element_type=F32)
        mu = jnp.mean(o, axis=-1, keepdims=True)
        ctr = o - mu
        var = jnp.mean(ctr * ctr, axis=-1, keepdims=True)
        gate = g_ref[0, sl, :]
        o_ref[0, sl, :] = (ctr * lax.rsqrt(var + GN_EPS)
                           * (gate * jax.nn.sigmoid(gate))).astype(o_ref.dtype)
        return state

    lax.fori_loop(0, seq // c, body, jnp.zeros((RET_KEY_DIM, LANES), F32))


def _retention(log_gamma, q, k, v, g):
    b, seq, _ = q.shape
    spec = pl.BlockSpec((1, seq, LANES), lambda bi, h: (bi, 0, h))
    return pl.pallas_call(
        _ret_kernel,
        grid=(b, RET_HEADS),
        in_specs=[pl.BlockSpec(memory_space=pltpu.SMEM), spec, spec, spec, spec],
        out_specs=spec,
        out_shape=jax.ShapeDtypeStruct((b, seq, RET_WIDTH), BF16),
        compiler_params=pltpu.CompilerParams(
            dimension_semantics=("arbitrary", "arbitrary"),
            vmem_limit_bytes=VMEM_LIMIT),
        name="retention",
    )(log_gamma, q, k, v, g)


def _merge_kernel(x_ref, osb_ref, or_ref, gsb_ref, gret_ref,
                  wsb_ref, wret_ref, wout_ref, g_ref, o_ref):
    p_sb = jnp.dot(osb_ref[...], wsb_ref[...], preferred_element_type=F32)
    p_ret = jnp.dot(or_ref[...], wret_ref[...], preferred_element_type=F32)
    merged = (jax.nn.sigmoid(gsb_ref[...]) * p_sb
              + jax.nn.sigmoid(gret_ref[...]) * p_ret)
    y = jnp.dot(merged.astype(BF16), wout_ref[...], preferred_element_type=F32)
    o_ref[...] = x_ref[...] + _rms(y, g_ref[...])


def _merge(x2d, o_sb, o_r, gate_sb, gate_ret, w_sb, w_ret, w_out, g_post):
    m = x2d.shape[0]

    def tile(width):
        return pl.BlockSpec((MIX_TM, width), lambda i: (i, 0))

    return pl.pallas_call(
        _merge_kernel,
        grid=(m // MIX_TM,),
        in_specs=[tile(D_MODEL), tile(SB_WIDTH), tile(RET_WIDTH),
                  tile(D_MODEL), tile(D_MODEL),
                  _resident((SB_WIDTH, D_MODEL)), _resident((RET_WIDTH, D_MODEL)),
                  _resident((D_MODEL, D_MODEL)),
                  pl.BlockSpec((1, D_MODEL), lambda i: (0, 0))],
        out_specs=tile(D_MODEL),
        out_shape=jax.ShapeDtypeStruct((m, D_MODEL), F32),
        compiler_params=pltpu.CompilerParams(
            dimension_semantics=("arbitrary",), vmem_limit_bytes=VMEM_LIMIT),
        name="merge",
    )(x2d, o_sb, o_r, gate_sb, gate_ret, w_sb, w_ret, w_out, g_post)


def _rope_tables(seq):
    half = RET_KEY_DIM // 2
    pos = jnp.arange(seq, dtype=F32)
    inv_freq = ROPE_BASE ** (-jnp.arange(half, dtype=F32) / half)
    ang = pos[:, None] * inv_freq[None, :]
    cos, sin = jnp.cos(ang), jnp.sin(ang)
    return (jnp.concatenate([cos, cos], axis=-1),
            jnp.concatenate([-sin, sin], axis=-1))


def _suffix_sum_matrix():
    j = np.arange(2 * SB_TK)[:, None] % SB_TK
    s = np.arange(2 * SB_TK)[None, :]
    return jnp.asarray((s >= SB_TK) | (j > s), dtype=BF16)


def _ffn_weights(w_gate, w_up, w_down):
    nj = D_FF // FFN_TF
    wg = w_gate.astype(BF16).reshape(D_MODEL, nj, FFN_TF).transpose(1, 0, 2)
    wu = w_up.astype(BF16).reshape(D_MODEL, nj, FFN_TF).transpose(1, 0, 2)
    wd = w_down.astype(BF16).reshape(nj, FFN_TF, D_MODEL)
    return wg, wu, wd


def kernel(x, g_ffn1_pre, g_ffn1_post, w_ffn1_gate, w_ffn1_up, w_ffn1_down,
           g_mix_pre, w_in, w_proj_sb, w_proj_ret, w_out, g_mix_post,
           g_ffn2_pre, g_ffn2_post, w_ffn2_gate, w_ffn2_up, w_ffn2_down):
    b, seq, d = x.shape
    depth = w_in.shape[0]
    log_gamma = jnp.log(1.0 - 2.0 ** (-5.0 - jnp.arange(RET_HEADS, dtype=F32)))
    cosf, sinf = _rope_tables(seq)
    tri = _suffix_sum_matrix()
    x2d = x.reshape(b * seq, d)
    for l in range(depth):
        x2d = _ffn(x2d, g_ffn1_pre[l][None], g_ffn1_post[l][None],
                   *_ffn_weights(w_ffn1_gate[l], w_ffn1_up[l], w_ffn1_down[l]))
        (q_sb, k_sb, v_sb, q_r, k_r, v_r, g_r, gate_sb, gate_ret) = _mix_in(
            x2d, g_mix_pre[l][None], w_in[l].astype(BF16), cosf, sinf, seq)
        shp = (b, seq, SB_WIDTH)
        o_sb = _sb_attention(q_sb.reshape(shp), k_sb.reshape(shp),
                             v_sb.reshape(shp), tri)
        o_r = _retention(log_gamma, q_r.reshape(shp), k_r.reshape(shp),
                         v_r.reshape(shp), g_r.reshape(shp))
        x2d = _merge(x2d, o_sb.reshape(b * seq, SB_WIDTH),
                     o_r.reshape(b * seq, RET_WIDTH), gate_sb, gate_ret,
                     w_proj_sb[l].astype(BF16), w_proj_ret[l].astype(BF16),
                     w_out[l].astype(BF16), g_mix_post[l][None])
        x2d = _ffn(x2d, g_ffn2_pre[l][None], g_ffn2_post[l][None],
                   *_ffn_weights(w_ffn2_gate[l], w_ffn2_up[l], w_ffn2_down[l]))
    return x2d.reshape(b, seq, d)
```

```python
import math

import numpy as np
import jax
import jax.numpy as jnp
from jax import lax
from jax.experimental import pallas as pl
from jax.experimental.pallas import tpu as pltpu

F32 = jnp.float32
BF16 = jnp.bfloat16

D_MODEL = 1024
D_FF = 4 * D_MODEL
SB_HEADS = 8
SB_HEAD_DIM = 64
SB_WIDTH = SB_HEADS * SB_HEAD_DIM
RET_HEADS = 4
RET_KEY_DIM = 128
RET_WIDTH = RET_HEADS * RET_KEY_DIM
RMS_EPS = 1e-6
GN_EPS = 1e-5
ROPE_BASE = 10000.0

LANES = 128
VMEM_LIMIT = 56 * 1024 * 1024

FFN_TM = 512
FFN_TF = 1024
MIX_TM = 512
SB_TQ = 256
SB_TK = LANES
SB_SPAN = SB_TQ
SB_NB = SB_SPAN // SB_TK
SB_Q_SCALE = SB_HEAD_DIM ** -0.5 * math.log2(math.e)
RET_CHUNK = 256


def _rms(x, g):
    ms = jnp.mean(x * x, axis=-1, keepdims=True)
    return x * lax.rsqrt(ms + RMS_EPS) * g


def _resident(shape):
    nd = len(shape)
    return pl.BlockSpec(shape, lambda *_: (0,) * nd, pipeline_mode=pl.Buffered(1))


def _ffn_kernel(x_ref, gpre_ref, gpost_ref, wg_ref, wu_ref, wd_ref, o_ref):
    x = x_ref[...]
    h = _rms(x, gpre_ref[...]).astype(BF16)
    acc = None
    for j in range(D_FF // FFN_TF):
        cols = slice(j * FFN_TF, (j + 1) * FFN_TF)
        g = jnp.dot(h, wg_ref[:, cols], preferred_element_type=F32)
        u = jnp.dot(h, wu_ref[:, cols], preferred_element_type=F32)
        a = (g * jax.nn.sigmoid(g) * u).astype(BF16)
        d = jnp.dot(a, wd_ref[cols, :], preferred_element_type=F32)
        acc = d if acc is None else acc + d
    o_ref[...] = x + 0.5 * _rms(acc, gpost_ref[...])


def _ffn(x2d, g_pre, g_post, wg, wu, wd):
    m = x2d.shape[0]
    tile = pl.BlockSpec((FFN_TM, D_MODEL), lambda i: (i, 0))
    gain = pl.BlockSpec((1, D_MODEL), lambda i: (0, 0))
    return pl.pallas_call(
        _ffn_kernel,
        grid=(m // FFN_TM,),
        in_specs=[tile, gain, gain,
                  _resident((D_MODEL, D_FF)), _resident((D_MODEL, D_FF)),
                  _resident((D_FF, D_MODEL))],
        out_specs=tile,
        out_shape=jax.ShapeDtypeStruct((m, D_MODEL), F32),
        compiler_params=pltpu.CompilerParams(
            dimension_semantics=("arbitrary",), vmem_limit_bytes=VMEM_LIMIT),
        name="ffn",
    )(x2d, g_pre, g_post, wg, wu, wd)


def _mix_in_kernel(x_ref, g_ref, w_ref, cos_ref, sin_ref,
                   qsb_ref, ksb_ref, vsb_ref, qr_ref, kr_ref, vr_ref,
                   gr_ref, gsb_ref, gret_ref):
    h = _rms(x_ref[...], g_ref[...]).astype(BF16)

    def mm(lo, width):
        return jnp.dot(h, w_ref[:, lo:lo + width], preferred_element_type=F32)

    def rotary(t):
        cosf = cos_ref[...]
        sinf = sin_ref[...]
        parts = []
        for hd in range(RET_HEADS):
            th = t[:, hd * LANES:(hd + 1) * LANES]
            parts.append(th * cosf + pltpu.roll(th, LANES // 2, axis=1) * sinf)
        return jnp.concatenate(parts, axis=1)

    c = 0
    qsb_ref[...] = (mm(c, SB_WIDTH) * SB_Q_SCALE).astype(BF16); c += SB_WIDTH
    ksb_ref[...] = mm(c, SB_WIDTH).astype(BF16); c += SB_WIDTH
    vsb_ref[...] = mm(c, SB_WIDTH).astype(BF16); c += SB_WIDTH
    qr_ref[...] = rotary(mm(c, RET_WIDTH)).astype(BF16); c += RET_WIDTH
    kr_ref[...] = (rotary(mm(c, RET_WIDTH)) * (RET_KEY_DIM ** -0.5)).astype(BF16); c += RET_WIDTH
    vr_ref[...] = mm(c, RET_WIDTH).astype(BF16); c += RET_WIDTH
    gr_ref[...] = mm(c, RET_WIDTH); c += RET_WIDTH
    gsb_ref[...] = mm(c, D_MODEL); c += D_MODEL
    gret_ref[...] = mm(c, D_MODEL)


def _mix_in(x2d, g, w_in, cosf, sinf, seq):
    m = x2d.shape[0]
    n_in = w_in.shape[1]
    pos_blocks = seq // MIX_TM

    def tile(width):
        return pl.BlockSpec((MIX_TM, width), lambda i: (i, 0))

    rope = pl.BlockSpec((MIX_TM, LANES), lambda i: (i % pos_blocks, 0))
    widths = [SB_WIDTH] * 3 + [RET_WIDTH] * 3 + [RET_WIDTH, D_MODEL, D_MODEL]
    dtypes = [BF16] * 6 + [F32] * 3
    return pl.pallas_call(
        _mix_in_kernel,
        grid=(m // MIX_TM,),
        in_specs=[tile(D_MODEL), pl.BlockSpec((1, D_MODEL), lambda i: (0, 0)),
                  _resident((D_MODEL, n_in)), rope, rope],
        out_specs=[tile(w) for w in widths],
        out_shape=[jax.ShapeDtypeStruct((m, w), dt) for w, dt in zip(widths, dtypes)],
        compiler_params=pltpu.CompilerParams(
            dimension_semantics=("arbitrary",), vmem_limit_bytes=VMEM_LIMIT),
        name="mix_in",
    )(x2d, g, w_in, cosf, sinf)


def _sb_kernel(dq_ref, dk_ref, eq_ref, ek_ref, q_ref, k_ref, v_ref, tri_ref, o_ref,
               pre_ref, tot_ref, later_ref, acc_ref):
    lane = lax.broadcasted_iota(jnp.int32, (SB_TQ, LANES), 1)
    head0 = lane < SB_HEAD_DIM
    row = lax.broadcasted_iota(jnp.int32, (SB_TQ, SB_TK), 0)
    col = lax.broadcasted_iota(jnp.int32, (SB_TQ, SB_TK), 1)
    rel = [col - row + c * SB_TK for c in range(SB_NB)]
    tri = tri_ref[...]

    def split_heads(x):
        zero = jnp.zeros_like(x)
        return jnp.where(head0, x, zero), jnp.where(head0, zero, x)

    def rows(ref, tile):
        return ref[0, pl.ds(pl.multiple_of(tile * SB_TQ, SB_TQ), SB_TQ), :]

    def qk(qt, kt):
        k = rows(k_ref, kt)
        return [lax.dot_general(qh, k, (((1,), (1,)), ((), ())),
                                preferred_element_type=F32)
                for qh in split_heads(rows(q_ref, qt))]

    def stage_sums(t, hd, slot, diagonal):
        sp = jnp.maximum(t, 0.0) + jnp.log2(1.0 + jnp.exp2(-jnp.abs(t)))
        for c in range(SB_NB):
            cols = slice(c * SB_TK, (c + 1) * SB_TK)
            spc = sp[:, cols]
            if diagonal:
                spc = jnp.where(rel[c] < 0, spc, 0.0)
            hi = spc.astype(BF16)
            lo = (spc - hi.astype(F32)).astype(BF16)
            sums = jnp.dot(jnp.concatenate([hi, lo], axis=1), tri,
                           preferred_element_type=F32)
            pre_ref[slot, hd, c] = t[:, cols] + sums[:, :SB_TK]
            tot_ref[slot, hd, c] = sums[:, SB_TK:]

    def weights(hd, slot, qt, diagonal):
        run = None if diagonal else later_ref[qt, hd]
        blocks = [None] * SB_NB
        for c in reversed(range(SB_NB)):
            arg = pre_ref[slot, hd, c]
            w = jnp.exp2(arg if run is None else arg + run)
            if diagonal:
                w = jnp.where(rel[c] < 0, w, 0.0)
            blocks[c] = w.astype(BF16)
            tot = tot_ref[slot, hd, c]
            run = tot if run is None else run + tot
        later_ref[qt, hd] = run
        return jnp.concatenate(blocks, axis=1)

    def weighted_values(w0, w1, qt, kt, diagonal):
        v_heads = jnp.concatenate(split_heads(rows(v_ref, kt)), axis=0)
        pv = jnp.dot(jnp.concatenate([w0, w1], axis=1), v_heads,
                     preferred_element_type=F32)
        if diagonal:
            acc_ref[qt] = pv
        else:
            acc_ref[qt] += pv

    def sweep(sq_ref, sk_ref, n_steps, diagonal):
        def overlapped(s, slot):
            qt, kt = sq_ref[s], sk_ref[s]
            t = qk(sq_ref[s + 1], sk_ref[s + 1])
            w0 = weights(0, slot, qt, diagonal)
            stage_sums(t[0], 0, 1 - slot, diagonal)
            w1 = weights(1, slot, qt, diagonal)
            weighted_values(w0, w1, qt, kt, diagonal)
            stage_sums(t[1], 1, 1 - slot, diagonal)

        t = qk(sq_ref[0], sk_ref[0])
        stage_sums(t[0], 0, 0, diagonal)
        stage_sums(t[1], 1, 0, diagonal)

        def body(i, carry):
            overlapped(2 * i, 0)
            overlapped(2 * i + 1, 1)
            return carry

        lax.fori_loop(0, n_steps // 2, body, 0)

    n_tiles = acc_ref.shape[0]
    sweep(dq_ref, dk_ref, n_tiles, True)
    sweep(eq_ref, ek_ref, n_tiles * (n_tiles - 1) // 2, False)
    for tile in range(n_tiles):
        o_ref[0, tile * SB_TQ:(tile + 1) * SB_TQ, :] = acc_ref[tile].astype(o_ref.dtype)


def _sb_steps(n_tiles):
    diag = [(t, t) for t in range(n_tiles)]
    early = [(t, kt) for t in range(1, n_tiles) for kt in range(t - 1, -1, -1)]
    assert len(diag) % 2 == 0 and len(early) % 2 == 0
    tables = []
    for steps in (diag, early):
        steps = steps + [steps[-1]]
        tables += [jnp.asarray([s[0] for s in steps], jnp.int32),
                   jnp.asarray([s[1] for s in steps], jnp.int32)]
    return tables


def _sb_attention(q, k, v, tri):
    b, seq, _ = q.shape
    pairs = SB_WIDTH // LANES
    n_tiles = seq // SB_TQ
    spec = pl.BlockSpec((1, seq, LANES), lambda bi, p: (bi, 0, p))
    smem = pl.BlockSpec(memory_space=pltpu.SMEM)
    staged = pltpu.VMEM((2, 2, SB_NB, SB_TQ, SB_TK), F32)
    return pl.pallas_call(
        _sb_kernel,
        grid=(b, pairs),
        in_specs=[smem, smem, smem, smem, spec, spec, spec,
                  pl.BlockSpec((2 * SB_TK, 2 * SB_TK), lambda bi, p: (0, 0))],
        out_specs=spec,
        out_shape=jax.ShapeDtypeStruct((b, seq, SB_WIDTH), BF16),
        scratch_shapes=[staged, staged,
                        pltpu.VMEM((n_tiles, 2, SB_TQ, SB_TK), F32),
                        pltpu.VMEM((n_tiles, SB_TQ, LANES), F32)],
        compiler_params=pltpu.CompilerParams(
            dimension_semantics=("arbitrary", "arbitrary"),
            vmem_limit_bytes=VMEM_LIMIT),
        name="sb_attn",
    )(*_sb_steps(n_tiles), q, k, v, tri)


def _ret_kernel(lg_ref, q_ref, k_ref, v_ref, g_ref, o_ref):
    seq = q_ref.shape[1]
    c = RET_CHUNK
    lg = lg_ref[pl.program_id(1)]
    diff = (lax.broadcasted_iota(jnp.int32, (c, c), 0)
            - lax.broadcasted_iota(jnp.int32, (c, c), 1)).astype(F32)
    inner_decay = jnp.where(diff >= 0, jnp.exp(jnp.maximum(diff, 0.0) * lg), 0.0)
    idx = lax.broadcasted_iota(jnp.int32, (c, 1), 0).astype(F32)
    q_decay = jnp.exp((idx + 1.0) * lg)
    k_decay = jnp.exp((c - 1.0 - idx) * lg)
    chunk_decay = jnp.exp(jnp.full((1, LANES), float(c), F32) * lg)

    def body(n, state):
        sl = pl.ds(pl.multiple_of(n * c, c), c)
        q = q_ref[0, sl, :]
        k = k_ref[0, sl, :]
        v = v_ref[0, sl, :]
        scores = lax.dot_general(q, k, (((1,), (1,)), ((), ())),
                                 preferred_element_type=F32) * inner_decay
        inner = jnp.dot(scores.astype(BF16), v, preferred_element_type=F32)
        cross = jnp.dot((q.astype(F32) * q_decay).astype(BF16), state.astype(BF16),
                        preferred_element_type=F32)
        o = inner + cross
        kd_t = (k.astype(F32) * k_decay).T.astype(BF16)
        state = chunk_decay * state + jnp.dot(kd_t, v, preferred_element_type=F32)
        mu = jnp.mean(o, axis=-1, keepdims=True)
        ctr = o - mu
        var = jnp.mean(ctr * ctr, axis=-1, keepdims=True)
        gate = g_ref[0, sl, :]
        o_ref[0, sl, :] = (ctr * lax.rsqrt(var + GN_EPS)
                           * (gate * jax.nn.sigmoid(gate))).astype(o_ref.dtype)
        return state

    lax.fori_loop(0, seq // c, body, jnp.zeros((RET_KEY_DIM, LANES), F32))


def _retention(log_gamma, q, k, v, g):
    b, seq, _ = q.shape
    spec = pl.BlockSpec((1, seq, LANES), lambda bi, h: (bi, 0, h))
    return pl.pallas_call(
        _ret_kernel,
        grid=(b, RET_HEADS),
        in_specs=[pl.BlockSpec(memory_space=pltpu.SMEM), spec, spec, spec, spec],
        out_specs=spec,
        out_shape=jax.ShapeDtypeStruct((b, seq, RET_WIDTH), BF16),
        compiler_params=pltpu.CompilerParams(
            dimension_semantics=("arbitrary", "arbitrary"),
            vmem_limit_bytes=VMEM_LIMIT),
        name="retention",
    )(log_gamma, q, k, v, g)


def _merge_kernel(x_ref, osb_ref, or_ref, gsb_ref, gret_ref,
                  wsb_ref, wret_ref, wout_ref, g_ref, o_ref):
    p_sb = jnp.dot(osb_ref[...], wsb_ref[...], preferred_element_type=F32)
    p_ret = jnp.dot(or_ref[...], wret_ref[...], preferred_element_type=F32)
    merged = (jax.nn.sigmoid(gsb_ref[...]) * p_sb
              + jax.nn.sigmoid(gret_ref[...]) * p_ret)
    y = jnp.dot(merged.astype(BF16), wout_ref[...], preferred_element_type=F32)
    o_ref[...] = x_ref[...] + _rms(y, g_ref[...])


def _merge(x2d, o_sb, o_r, gate_sb, gate_ret, w_sb, w_ret, w_out, g_post):
    m = x2d.shape[0]

    def tile(width):
        return pl.BlockSpec((MIX_TM, width), lambda i: (i, 0))

    return pl.pallas_call(
        _merge_kernel,
        grid=(m // MIX_TM,),
        in_specs=[tile(D_MODEL), tile(SB_WIDTH), tile(RET_WIDTH),
                  tile(D_MODEL), tile(D_MODEL),
                  _resident((SB_WIDTH, D_MODEL)), _resident((RET_WIDTH, D_MODEL)),
                  _resident((D_MODEL, D_MODEL)),
                  pl.BlockSpec((1, D_MODEL), lambda i: (0, 0))],
        out_specs=tile(D_MODEL),
        out_shape=jax.ShapeDtypeStruct((m, D_MODEL), F32),
        compiler_params=pltpu.CompilerParams(
            dimension_semantics=("arbitrary",), vmem_limit_bytes=VMEM_LIMIT),
        name="merge",
    )(x2d, o_sb, o_r, gate_sb, gate_ret, w_sb, w_ret, w_out, g_post)


def _rope_tables(seq):
    half = RET_KEY_DIM // 2
    pos = jnp.arange(seq, dtype=F32)
    inv_freq = ROPE_BASE ** (-jnp.arange(half, dtype=F32) / half)
    ang = pos[:, None] * inv_freq[None, :]
    cos, sin = jnp.cos(ang), jnp.sin(ang)
    return (jnp.concatenate([cos, cos], axis=-1),
            jnp.concatenate([-sin, sin], axis=-1))


def _suffix_sum_matrix():
    j = np.arange(2 * SB_TK)[:, None] % SB_TK
    s = np.arange(2 * SB_TK)[None, :]
    return jnp.asarray(-((s >= SB_TK) | (j >= s)).astype(np.float32), dtype=BF16)


def kernel(x, g_ffn1_pre, g_ffn1_post, w_ffn1_gate, w_ffn1_up, w_ffn1_down,
           g_mix_pre, w_in, w_proj_sb, w_proj_ret, w_out, g_mix_post,
           g_ffn2_pre, g_ffn2_post, w_ffn2_gate, w_ffn2_up, w_ffn2_down):
    b, seq, d = x.shape
    depth = w_in.shape[0]
    log_gamma = jnp.log(1.0 - 2.0 ** (-5.0 - jnp.arange(RET_HEADS, dtype=F32)))
    cosf, sinf = _rope_tables(seq)
    tri = _suffix_sum_matrix()
    x2d = x.reshape(b * seq, d)
    for l in range(depth):
        x2d = _ffn(x2d, g_ffn1_pre[l][None], g_ffn1_post[l][None],
                   w_ffn1_gate[l].astype(BF16), w_ffn1_up[l].astype(BF16),
                   w_ffn1_down[l].astype(BF16))
        (q_sb, k_sb, v_sb, q_r, k_r, v_r, g_r, gate_sb, gate_ret) = _mix_in(
            x2d, g_mix_pre[l][None], w_in[l].astype(BF16), cosf, sinf, seq)
        shp = (b, seq, SB_WIDTH)
        o_sb = _sb_attention(q_sb.reshape(shp), k_sb.reshape(shp),
                             v_sb.reshape(shp), tri)
        o_r = _retention(log_gamma, q_r.reshape(shp), k_r.reshape(shp),
                         v_r.reshape(shp), g_r.reshape(shp))
        x2d = _merge(x2d, o_sb.reshape(b * seq, SB_WIDTH),
                     o_r.reshape(b * seq, RET_WIDTH), gate_sb, gate_ret,
                     w_proj_sb[l].astype(BF16), w_proj_ret[l].astype(BF16),
                     w_out[l].astype(BF16), g_mix_post[l][None])
        x2d = _ffn(x2d, g_ffn2_pre[l][None], g_ffn2_post[l][None],
                   w_ffn2_gate[l].astype(BF16), w_ffn2_up[l].astype(BF16),
                   w_ffn2_down[l].astype(BF16))
    return x2d.reshape(b, seq, d)
```

```python
import math

import numpy as np
import jax
import jax.numpy as jnp
from jax import lax
from jax.experimental import pallas as pl
from jax.experimental.pallas import tpu as pltpu

F32 = jnp.float32
BF16 = jnp.bfloat16

D_MODEL = 1024
D_FF = 4 * D_MODEL
SB_HEADS = 8
SB_HEAD_DIM = 64
SB_WIDTH = SB_HEADS * SB_HEAD_DIM
RET_HEADS = 4
RET_KEY_DIM = 128
RET_WIDTH = RET_HEADS * RET_KEY_DIM
RMS_EPS = 1e-6
GN_EPS = 1e-5
ROPE_BASE = 10000.0

LANES = 128
VMEM_LIMIT = 56 * 1024 * 1024

FFN_TM = 512
FFN_TF = 1024
MIX_TM = 512
SB_TQ = 256
SB_TK = LANES
SB_SPAN = SB_TQ
SB_NB = SB_SPAN // SB_TK
SB_Q_SCALE = SB_HEAD_DIM ** -0.5 * math.log2(math.e)
RET_CHUNK = 256


def _rms(x, g):
    ms = jnp.mean(x * x, axis=-1, keepdims=True)
    return x * lax.rsqrt(ms + RMS_EPS) * g


def _resident(shape):
    nd = len(shape)
    return pl.BlockSpec(shape, lambda *_: (0,) * nd, pipeline_mode=pl.Buffered(1))


def _ffn_kernel(x_ref, gpre_ref, gpost_ref, wg_ref, wu_ref, wd_ref, o_ref):
    x = x_ref[...]
    h = _rms(x, gpre_ref[...]).astype(BF16)
    acc = None
    for j in range(D_FF // FFN_TF):
        cols = slice(j * FFN_TF, (j + 1) * FFN_TF)
        g = jnp.dot(h, wg_ref[:, cols], preferred_element_type=F32)
        u = jnp.dot(h, wu_ref[:, cols], preferred_element_type=F32)
        a = (g * jax.nn.sigmoid(g) * u).astype(BF16)
        d = jnp.dot(a, wd_ref[cols, :], preferred_element_type=F32)
        acc = d if acc is None else acc + d
    o_ref[...] = x + 0.5 * _rms(acc, gpost_ref[...])


def _ffn(x2d, g_pre, g_post, wg, wu, wd):
    m = x2d.shape[0]
    tile = pl.BlockSpec((FFN_TM, D_MODEL), lambda i: (i, 0))
    gain = pl.BlockSpec((1, D_MODEL), lambda i: (0, 0))
    return pl.pallas_call(
        _ffn_kernel,
        grid=(m // FFN_TM,),
        in_specs=[tile, gain, gain,
                  _resident((D_MODEL, D_FF)), _resident((D_MODEL, D_FF)),
                  _resident((D_FF, D_MODEL))],
        out_specs=tile,
        out_shape=jax.ShapeDtypeStruct((m, D_MODEL), F32),
        compiler_params=pltpu.CompilerParams(
            dimension_semantics=("arbitrary",), vmem_limit_bytes=VMEM_LIMIT),
        name="ffn",
    )(x2d, g_pre, g_post, wg, wu, wd)


def _mix_in_kernel(x_ref, g_ref, w_ref, cos_ref, sin_ref,
                   qsb_ref, ksb_ref, vsb_ref, qr_ref, kr_ref, vr_ref,
                   gr_ref, gsb_ref, gret_ref):
    h = _rms(x_ref[...], g_ref[...]).astype(BF16)

    def mm(lo, width):
        return jnp.dot(h, w_ref[:, lo:lo + width], preferred_element_type=F32)

    def rotary(t):
        cosf = cos_ref[...]
        sinf = sin_ref[...]
        parts = []
        for hd in range(RET_HEADS):
            th = t[:, hd * LANES:(hd + 1) * LANES]
            parts.append(th * cosf + pltpu.roll(th, LANES // 2, axis=1) * sinf)
        return jnp.concatenate(parts, axis=1)

    c = 0
    qsb_ref[...] = (mm(c, SB_WIDTH) * SB_Q_SCALE).astype(BF16); c += SB_WIDTH
    ksb_ref[...] = mm(c, SB_WIDTH).astype(BF16); c += SB_WIDTH
    vsb_ref[...] = mm(c, SB_WIDTH).astype(BF16); c += SB_WIDTH
    qr_ref[...] = rotary(mm(c, RET_WIDTH)).astype(BF16); c += RET_WIDTH
    kr_ref[...] = (rotary(mm(c, RET_WIDTH)) * (RET_KEY_DIM ** -0.5)).astype(BF16); c += RET_WIDTH
    vr_ref[...] = mm(c, RET_WIDTH).astype(BF16); c += RET_WIDTH
    gr_ref[...] = mm(c, RET_WIDTH); c += RET_WIDTH
    gsb_ref[...] = mm(c, D_MODEL); c += D_MODEL
    gret_ref[...] = mm(c, D_MODEL)


def _mix_in(x2d, g, w_in, cosf, sinf, seq):
    m = x2d.shape[0]
    n_in = w_in.shape[1]
    pos_blocks = seq // MIX_TM

    def tile(width):
        return pl.BlockSpec((MIX_TM, width), lambda i: (i, 0))

    rope = pl.BlockSpec((MIX_TM, LANES), lambda i: (i % pos_blocks, 0))
    widths = [SB_WIDTH] * 3 + [RET_WIDTH] * 3 + [RET_WIDTH, D_MODEL, D_MODEL]
    dtypes = [BF16] * 6 + [F32] * 3
    return pl.pallas_call(
        _mix_in_kernel,
        grid=(m // MIX_TM,),
        in_specs=[tile(D_MODEL), pl.BlockSpec((1, D_MODEL), lambda i: (0, 0)),
                  _resident((D_MODEL, n_in)), rope, rope],
        out_specs=[tile(w) for w in widths],
        out_shape=[jax.ShapeDtypeStruct((m, w), dt) for w, dt in zip(widths, dtypes)],
        compiler_params=pltpu.CompilerParams(
            dimension_semantics=("arbitrary",), vmem_limit_bytes=VMEM_LIMIT),
        name="mix_in",
    )(x2d, g, w_in, cosf, sinf)


def _sb_kernel(dq_ref, dk_ref, eq_ref, ek_ref, q_ref, k_ref, v_ref, tri_ref, o_ref,
               qh_ref, vh_ref, t_ref, pre_ref, tot_ref, w_ref, later_ref, acc_ref):
    n_tiles = later_ref.shape[0]
    lane = lax.broadcasted_iota(jnp.int32, (SB_TQ, LANES), 1)
    head0 = lane < SB_HEAD_DIM
    row = lax.broadcasted_iota(jnp.int32, (SB_TQ, SB_TK), 0)
    col = lax.broadcasted_iota(jnp.int32, (SB_TQ, SB_TK), 1)
    rel = [col - row + c * SB_TK for c in range(SB_NB)]
    tri = tri_ref[...]

    def tile_rows(tile):
        return pl.ds(pl.multiple_of(tile * SB_TQ, SB_TQ), SB_TQ)

    for tile in range(n_tiles):
        sl = slice(tile * SB_TQ, (tile + 1) * SB_TQ)
        q = q_ref[0, sl, :]
        v = v_ref[0, sl, :]
        zero = jnp.zeros_like(q)
        qh_ref[0, sl, :] = jnp.where(head0, q, zero)
        qh_ref[1, sl, :] = jnp.where(head0, zero, q)
        vh_ref[tile, :SB_SPAN, :] = jnp.where(head0, v, zero)
        vh_ref[tile, SB_SPAN:, :] = jnp.where(head0, zero, v)
    w_ref[1] = jnp.zeros(w_ref.shape[1:], w_ref.dtype)

    def scores(hd, qt, kt):
        return lax.dot_general(qh_ref[hd, tile_rows(qt), :], k_ref[0, tile_rows(kt), :],
                               (((1,), (1,)), ((), ())), preferred_element_type=F32)

    def suffix_sums(t, diagonal):
        sp = jnp.maximum(t, 0.0) + jnp.log2(1.0 + jnp.exp2(-jnp.abs(t)))
        sums = []
        for c in range(SB_NB):
            spc = sp[:, c * SB_TK:(c + 1) * SB_TK]
            if diagonal:
                spc = jnp.where(rel[c] < 0, spc, 0.0)
            hi = spc.astype(BF16)
            lo = (spc - hi.astype(F32)).astype(BF16)
            sums.append(jnp.dot(jnp.concatenate([hi, lo], axis=1), tri,
                                preferred_element_type=F32))
        return sums

    def stage_sums(t, sums, hd, slot):
        for c in range(SB_NB):
            pre_ref[slot, hd, c] = t[:, c * SB_TK:(c + 1) * SB_TK] + sums[c][:, :SB_TK]
            tot_ref[slot, hd, c] = sums[c][:, SB_TK:]

    def weights(hd, slot, qt, diagonal):
        run = None if diagonal else later_ref[qt, hd]
        for c in reversed(range(SB_NB)):
            arg = pre_ref[slot, hd, c]
            w = jnp.exp2(arg if run is None else arg + run)
            if diagonal:
                w = jnp.where(rel[c] < 0, w, 0.0)
            lo = hd * SB_SPAN + c * SB_TK
            w_ref[slot, :, lo:lo + SB_TK] = w.astype(BF16)
            tot = tot_ref[slot, hd, c]
            run = tot if run is None else run + tot
        later_ref[qt, hd] = run

    def sweep(sq_ref, sk_ref, n_steps, diagonal):
        def step(s):
            return sq_ref[s + 1], sk_ref[s + 1]

        def weighted_values(s, slot):
            qt, kt = step(s)
            pv = jnp.dot(w_ref[slot], vh_ref[kt], preferred_element_type=F32)
            return qt, pv

        def accumulate(qt, pv):
            if diagonal:
                acc_ref[qt] = pv
            else:
                acc_ref[qt] += pv

        def half(s, slot):
            t = [t_ref[1 - slot, hd] for hd in range(2)]
            t_new = [scores(0, *step(s + 2))]
            sums = [suffix_sums(t[0], diagonal)]
            qt_prev, pv = weighted_values(s - 1, 1 - slot)
            sums.append(suffix_sums(t[1], diagonal))
            t_new.append(scores(1, *step(s + 2)))
            qt, _ = step(s)
            for hd in range(2):
                weights(hd, slot, qt, diagonal)
            for hd in range(2):
                t_ref[slot, hd] = t_new[hd]
            accumulate(qt_prev, pv)
            for hd in range(2):
                stage_sums(t[hd], sums[hd], hd, 1 - slot)

        for hd in range(2):
            t0 = scores(hd, *step(0))
            stage_sums(t0, suffix_sums(t0, diagonal), hd, 0)
        for hd in range(2):
            t_ref[1, hd] = scores(hd, *step(1))

        def body(i, carry):
            half(2 * i, 0)
            half(2 * i + 1, 1)
            return carry

        lax.fori_loop(0, n_steps // 2, body, 0)
        accumulate(*weighted_values(n_steps - 1, 1))

    sweep(dq_ref, dk_ref, n_tiles, True)
    sweep(eq_ref, ek_ref, n_tiles * (n_tiles - 1) // 2, False)
    for tile in range(n_tiles):
        o_ref[0, tile * SB_TQ:(tile + 1) * SB_TQ, :] = acc_ref[tile].astype(o_ref.dtype)


def _sb_steps(n_tiles):
    diag = [(t, t) for t in range(n_tiles)]
    early = [(t, kt) for t in range(1, n_tiles) for kt in range(t - 1, -1, -1)]
    assert len(diag) % 2 == 0 and len(early) % 2 == 0
    tables = []
    for steps in (diag, early):
        steps = [(n_tiles, 0)] + steps + [steps[-1]] * 2
        tables += [jnp.asarray([s[0] for s in steps], jnp.int32),
                   jnp.asarray([s[1] for s in steps], jnp.int32)]
    return tables


def _sb_attention(q, k, v, tri):
    b, seq, _ = q.shape
    pairs = SB_WIDTH // LANES
    n_tiles = seq // SB_TQ
    spec = pl.BlockSpec((1, seq, LANES), lambda bi, p: (bi, 0, p))
    smem = pl.BlockSpec(memory_space=pltpu.SMEM)
    staged = pltpu.VMEM((2, 2, SB_NB, SB_TQ, SB_TK), F32)
    return pl.pallas_call(
        _sb_kernel,
        grid=(b, pairs),
        in_specs=[smem, smem, smem, smem, spec, spec, spec,
                  pl.BlockSpec((2 * SB_TK, 2 * SB_TK), lambda bi, p: (0, 0))],
        out_specs=spec,
        out_shape=jax.ShapeDtypeStruct((b, seq, SB_WIDTH), BF16),
        scratch_shapes=[pltpu.VMEM((2, seq, LANES), BF16),
                        pltpu.VMEM((n_tiles, 2 * SB_SPAN, LANES), BF16),
                        pltpu.VMEM((2, 2, SB_TQ, SB_SPAN), F32),
                        staged, staged,
                        pltpu.VMEM((2, SB_TQ, 2 * SB_SPAN), BF16),
                        pltpu.VMEM((n_tiles, 2, SB_TQ, SB_TK), F32),
                        pltpu.VMEM((n_tiles + 1, SB_TQ, LANES), F32)],
        compiler_params=pltpu.CompilerParams(
            dimension_semantics=("arbitrary", "arbitrary"),
            vmem_limit_bytes=VMEM_LIMIT),
        name="sb_attn",
    )(*_sb_steps(n_tiles), q, k, v, tri)


def _ret_kernel(lg_ref, q_ref, k_ref, v_ref, g_ref, o_ref,
                decay_ref, qdec_ref, kdec_ref, state_ref):
    seq = q_ref.shape[1]
    c = RET_CHUNK
    heads = range(RET_HEADS)
    diff = (lax.broadcasted_iota(jnp.int32, (c, c), 0)
            - lax.broadcasted_iota(jnp.int32, (c, c), 1)).astype(F32)
    idx = lax.broadcasted_iota(jnp.int32, (c, LANES), 0).astype(F32)
    for h in heads:
        lg = lg_ref[h]
        decay_ref[h] = jnp.where(diff >= 0, jnp.exp(jnp.maximum(diff, 0.0) * lg), 0.0)
        qdec_ref[h] = jnp.exp((idx + 1.0) * lg)
        kdec_ref[h] = jnp.exp((c - 1.0 - idx) * lg)
    state_ref[...] = jnp.zeros(state_ref.shape, F32)

    def body(n, carry):
        sl = pl.ds(pl.multiple_of(n * c, c), c)
        q, k, v = ([ref[0, sl, h * LANES:(h + 1) * LANES] for h in heads]
                   for ref in (q_ref, k_ref, v_ref))
        state = [state_ref[h] for h in heads]
        scores = [lax.dot_general(q[h], k[h], (((1,), (1,)), ((), ())),
                                  preferred_element_type=F32) for h in heads]
        cross = [jnp.dot((q[h].astype(F32) * qdec_ref[h]).astype(BF16),
                         state[h].astype(BF16), preferred_element_type=F32)
                 for h in heads]
        kd_t = [(k[h].astype(F32) * kdec_ref[h]).T.astype(BF16) for h in heads]
        kv = [jnp.dot(kd_t[h], v[h], preferred_element_type=F32) for h in heads]
        inner = [jnp.dot((scores[h] * decay_ref[h]).astype(BF16), v[h],
                         preferred_element_type=F32) for h in heads]
        for h in heads:
            chunk_decay = jnp.exp(jnp.full((1, LANES), float(c), F32) * lg_ref[h])
            state_ref[h] = chunk_decay * state[h] + kv[h]
        for h in heads:
            o = inner[h] + cross[h]
            mu = jnp.mean(o, axis=-1, keepdims=True)
            ctr = o - mu
            var = jnp.mean(ctr * ctr, axis=-1, keepdims=True)
            gate = g_ref[0, sl, h * LANES:(h + 1) * LANES]
            o_ref[0, sl, h * LANES:(h + 1) * LANES] = (
                ctr * lax.rsqrt(var + GN_EPS)
                * (gate * jax.nn.sigmoid(gate))).astype(o_ref.dtype)
        return carry

    lax.fori_loop(0, seq // c, body, 0)


def _retention(log_gamma, q, k, v, g):
    b, seq, width = q.shape
    spec = pl.BlockSpec((1, seq, width), lambda bi: (bi, 0, 0))
    c = RET_CHUNK
    return pl.pallas_call(
        _ret_kernel,
        grid=(b,),
        in_specs=[pl.BlockSpec(memory_space=pltpu.SMEM), spec, spec, spec, spec],
        out_specs=spec,
        out_shape=jax.ShapeDtypeStruct((b, seq, RET_WIDTH), BF16),
        scratch_shapes=[pltpu.VMEM((RET_HEADS, c, c), F32),
                        pltpu.VMEM((RET_HEADS, c, LANES), F32),
                        pltpu.VMEM((RET_HEADS, c, LANES), F32),
                        pltpu.VMEM((RET_HEADS, RET_KEY_DIM, LANES), F32)],
        compiler_params=pltpu.CompilerParams(
            dimension_semantics=("arbitrary",), vmem_limit_bytes=VMEM_LIMIT),
        name="retention",
    )(log_gamma, q, k, v, g)


def _merge_kernel(x_ref, osb_ref, or_ref, gsb_ref, gret_ref,
                  wsb_ref, wret_ref, wout_ref, g_ref, o_ref):
    p_sb = jnp.dot(osb_ref[...], wsb_ref[...], preferred_element_type=F32)
    p_ret = jnp.dot(or_ref[...], wret_ref[...], preferred_element_type=F32)
    merged = (jax.nn.sigmoid(gsb_ref[...]) * p_sb
              + jax.nn.sigmoid(gret_ref[...]) * p_ret)
    y = jnp.dot(merged.astype(BF16), wout_ref[...], preferred_element_type=F32)
    o_ref[...] = x_ref[...] + _rms(y, g_ref[...])


def _merge(x2d, o_sb, o_r, gate_sb, gate_ret, w_sb, w_ret, w_out, g_post):
    m = x2d.shape[0]

    def tile(width):
        return pl.BlockSpec((MIX_TM, width), lambda i: (i, 0))

    return pl.pallas_call(
        _merge_kernel,
        grid=(m // MIX_TM,),
        in_specs=[tile(D_MODEL), tile(SB_WIDTH), tile(RET_WIDTH),
                  tile(D_MODEL), tile(D_MODEL),
                  _resident((SB_WIDTH, D_MODEL)), _resident((RET_WIDTH, D_MODEL)),
                  _resident((D_MODEL, D_MODEL)),
                  pl.BlockSpec((1, D_MODEL), lambda i: (0, 0))],
        out_specs=tile(D_MODEL),
        out_shape=jax.ShapeDtypeStruct((m, D_MODEL), F32),
        compiler_params=pltpu.CompilerParams(
            dimension_semantics=("arbitrary",), vmem_limit_bytes=VMEM_LIMIT),
        name="merge",
    )(x2d, o_sb, o_r, gate_sb, gate_ret, w_sb, w_ret, w_out, g_post)


def _rope_tables(seq):
    half = RET_KEY_DIM // 2
    pos = jnp.arange(seq, dtype=F32)
    inv_freq = ROPE_BASE ** (-jnp.arange(half, dtype=F32) / half)
    ang = pos[:, None] * inv_freq[None, :]
    cos, sin = jnp.cos(ang), jnp.sin(ang)
    return (jnp.concatenate([cos, cos], axis=-1),
            jnp.concatenate([-sin, sin], axis=-1))


def _suffix_sum_matrix():
    j = np.arange(2 * SB_TK)[:, None] % SB_TK
    s = np.arange(2 * SB_TK)[None, :]
    return jnp.asarray(-((s >= SB_TK) | (j >= s)).astype(np.float32), dtype=BF16)


def kernel(x, g_ffn1_pre, g_ffn1_post, w_ffn1_gate, w_ffn1_up, w_ffn1_down,
           g_mix_pre, w_in, w_proj_sb, w_proj_ret, w_out, g_mix_post,
           g_ffn2_pre, g_ffn2_post, w_ffn2_gate, w_ffn2_up, w_ffn2_down):
    b, seq, d = x.shape
    depth = w_in.shape[0]
    log_gamma = jnp.log(1.0 - 2.0 ** (-5.0 - jnp.arange(RET_HEADS, dtype=F32)))
    cosf, sinf = _rope_tables(seq)
    tri = _suffix_sum_matrix()
    x2d = x.reshape(b * seq, d)
    for l in range(depth):
        x2d = _ffn(x2d, g_ffn1_pre[l][None], g_ffn1_post[l][None],
                   w_ffn1_gate[l].astype(BF16), w_ffn1_up[l].astype(BF16),
                   w_ffn1_down[l].astype(BF16))
        (q_sb, k_sb, v_sb, q_r, k_r, v_r, g_r, gate_sb, gate_ret) = _mix_in(
            x2d, g_mix_pre[l][None], w_in[l].astype(BF16), cosf, sinf, seq)
        shp = (b, seq, SB_WIDTH)
        o_sb = _sb_attention(q_sb.reshape(shp), k_sb.reshape(shp),
                             v_sb.reshape(shp), tri)
        o_r = _retention(log_gamma, q_r.reshape(shp), k_r.reshape(shp),
                         v_r.reshape(shp), g_r.reshape(shp))
        x2d = _merge(x2d, o_sb.reshape(b * seq, SB_WIDTH),
                     o_r.reshape(b * seq, RET_WIDTH), gate_sb, gate_ret,
                     w_proj_sb[l].astype(BF16), w_proj_ret[l].astype(BF16),
                     w_out[l].astype(BF16), g_mix_post[l][None])
        x2d = _ffn(x2d, g_ffn2_pre[l][None], g_ffn2_post[l][None],
                   w_ffn2_gate[l].astype(BF16), w_ffn2_up[l].astype(BF16),
                   w_ffn2_down[l].astype(BF16))
    return x2d.reshape(b, seq, d)
```

```python
import math

import numpy as np
import jax
import jax.numpy as jnp
from jax import lax
from jax.experimental import pallas as pl
from jax.experimental.pallas import tpu as pltpu

F32 = jnp.float32
BF16 = jnp.bfloat16

D_MODEL = 1024
D_FF = 4 * D_MODEL
SB_HEADS = 8
SB_HEAD_DIM = 64
SB_WIDTH = SB_HEADS * SB_HEAD_DIM
RET_HEADS = 4
RET_KEY_DIM = 128
RET_WIDTH = RET_HEADS * RET_KEY_DIM
RMS_EPS = 1e-6
GN_EPS = 1e-5
ROPE_BASE = 10000.0

LANES = 128
VMEM_LIMIT = 56 * 1024 * 1024

FFN_TM = 512
FFN_TF = 1024
MIX_TM = 512
SB_TQ = 256
SB_TK = LANES
SB_SPAN = SB_TQ
SB_NB = SB_SPAN // SB_TK
SB_UNROLL = 4
SB_Q_SCALE = SB_HEAD_DIM ** -0.5 * math.log2(math.e)
RET_CHUNK = 256


def _rms(x, g):
    ms = jnp.mean(x * x, axis=-1, keepdims=True)
    return x * lax.rsqrt(ms + RMS_EPS) * g


def _resident(shape):
    nd = len(shape)
    return pl.BlockSpec(shape, lambda *_: (0,) * nd, pipeline_mode=pl.Buffered(1))


def _ffn_kernel(x_ref, gpre_ref, gpost_ref, wg_ref, wu_ref, wd_ref, o_ref):
    x = x_ref[...]
    h = _rms(x, gpre_ref[...]).astype(BF16)
    acc = None
    for j in range(D_FF // FFN_TF):
        cols = slice(j * FFN_TF, (j + 1) * FFN_TF)
        g = jnp.dot(h, wg_ref[:, cols], preferred_element_type=F32)
        u = jnp.dot(h, wu_ref[:, cols], preferred_element_type=F32)
        a = (g * jax.nn.sigmoid(g) * u).astype(BF16)
        d = jnp.dot(a, wd_ref[cols, :], preferred_element_type=F32)
        acc = d if acc is None else acc + d
    o_ref[...] = x + 0.5 * _rms(acc, gpost_ref[...])


def _ffn(x2d, g_pre, g_post, wg, wu, wd):
    m = x2d.shape[0]
    tile = pl.BlockSpec((FFN_TM, D_MODEL), lambda i: (i, 0))
    gain = pl.BlockSpec((1, D_MODEL), lambda i: (0, 0))
    return pl.pallas_call(
        _ffn_kernel,
        grid=(m // FFN_TM,),
        in_specs=[tile, gain, gain,
                  _resident((D_MODEL, D_FF)), _resident((D_MODEL, D_FF)),
                  _resident((D_FF, D_MODEL))],
        out_specs=tile,
        out_shape=jax.ShapeDtypeStruct((m, D_MODEL), F32),
        compiler_params=pltpu.CompilerParams(
            dimension_semantics=("arbitrary",), vmem_limit_bytes=VMEM_LIMIT),
        name="ffn",
    )(x2d, g_pre, g_post, wg, wu, wd)


def _mix_in_kernel(x_ref, g_ref, w_ref, cos_ref, sin_ref,
                   qsb_ref, ksb_ref, vsb_ref, qr_ref, kr_ref, vr_ref, gr_ref):
    h = _rms(x_ref[...], g_ref[...]).astype(BF16)

    def mm(lo, width):
        return jnp.dot(h, w_ref[:, lo:lo + width], preferred_element_type=F32)

    def rotary(t):
        cosf = cos_ref[...]
        sinf = sin_ref[...]
        parts = []
        for hd in range(RET_HEADS):
            th = t[:, hd * LANES:(hd + 1) * LANES]
            parts.append(th * cosf + pltpu.roll(th, LANES // 2, axis=1) * sinf)
        return jnp.concatenate(parts, axis=1)

    c = 0
    qsb_ref[...] = (mm(c, SB_WIDTH) * SB_Q_SCALE).astype(BF16); c += SB_WIDTH
    ksb_ref[...] = mm(c, SB_WIDTH).astype(BF16); c += SB_WIDTH
    vsb_ref[...] = mm(c, SB_WIDTH).astype(BF16); c += SB_WIDTH
    qr_ref[...] = rotary(mm(c, RET_WIDTH)).astype(BF16); c += RET_WIDTH
    kr_ref[...] = (rotary(mm(c, RET_WIDTH)) * (RET_KEY_DIM ** -0.5)).astype(BF16); c += RET_WIDTH
    vr_ref[...] = mm(c, RET_WIDTH).astype(BF16); c += RET_WIDTH
    gr_ref[...] = mm(c, RET_WIDTH)


def _mix_in(x2d, g, w_in, cosf, sinf, seq):
    m = x2d.shape[0]
    n_in = w_in.shape[1]
    pos_blocks = seq // MIX_TM

    def tile(width):
        return pl.BlockSpec((MIX_TM, width), lambda i: (i, 0))

    rope = pl.BlockSpec((MIX_TM, LANES), lambda i: (i % pos_blocks, 0))
    widths = [SB_WIDTH] * 3 + [RET_WIDTH] * 4
    dtypes = [BF16] * 6 + [F32]
    return pl.pallas_call(
        _mix_in_kernel,
        grid=(m // MIX_TM,),
        in_specs=[tile(D_MODEL), pl.BlockSpec((1, D_MODEL), lambda i: (0, 0)),
                  _resident((D_MODEL, n_in)), rope, rope],
        out_specs=[tile(w) for w in widths],
        out_shape=[jax.ShapeDtypeStruct((m, w), dt) for w, dt in zip(widths, dtypes)],
        compiler_params=pltpu.CompilerParams(
            dimension_semantics=("arbitrary",), vmem_limit_bytes=VMEM_LIMIT),
        name="mix_in",
    )(x2d, g, w_in, cosf, sinf)


def _sb_kernel(dq_ref, dk_ref, eq_ref, ek_ref, q_ref, k_ref, v_ref, tri_ref, o_ref,
               qh_ref, vh_ref, t_ref, pre_ref, tot_ref, w_ref, later_ref, acc_ref):
    n_tiles = later_ref.shape[0]
    lane = lax.broadcasted_iota(jnp.int32, (SB_TQ, LANES), 1)
    head0 = lane < SB_HEAD_DIM
    row = lax.broadcasted_iota(jnp.int32, (SB_TQ, SB_TK), 0)
    col = lax.broadcasted_iota(jnp.int32, (SB_TQ, SB_TK), 1)
    rel = [col - row + c * SB_TK for c in range(SB_NB)]
    tri = tri_ref[...]

    def tile_rows(tile):
        return pl.ds(pl.multiple_of(tile * SB_TQ, SB_TQ), SB_TQ)

    for tile in range(n_tiles):
        sl = slice(tile * SB_TQ, (tile + 1) * SB_TQ)
        q = q_ref[0, sl, :]
        v = v_ref[0, sl, :]
        zero = jnp.zeros_like(q)
        qh_ref[0, sl, :] = jnp.where(head0, q, zero)
        qh_ref[1, sl, :] = jnp.where(head0, zero, q)
        vh_ref[tile, :SB_SPAN, :] = jnp.where(head0, v, zero)
        vh_ref[tile, SB_SPAN:, :] = jnp.where(head0, zero, v)
    w_ref[1] = jnp.zeros(w_ref.shape[1:], w_ref.dtype)

    def scores(hd, qt, kt):
        return lax.dot_general(qh_ref[hd, tile_rows(qt), :], k_ref[0, tile_rows(kt), :],
                               (((1,), (1,)), ((), ())), preferred_element_type=F32)

    def suffix_sums(t, diagonal):
        sp = jnp.maximum(t, 0.0) + jnp.log2(1.0 + jnp.exp2(-jnp.abs(t)))
        if diagonal:
            sp = jnp.concatenate(
                [jnp.where(rel[c] < 0, sp[:, c * SB_TK:(c + 1) * SB_TK], 0.0)
                 for c in range(SB_NB)], axis=1)
        return jnp.dot(sp.astype(BF16), tri, preferred_element_type=F32)

    def stage_sums(t, sums, hd, slot):
        pre_ref[slot, hd] = t + sums
        tot_ref[slot, hd] = jnp.broadcast_to(sums[:, 0:1], (SB_TQ, SB_TK))

    def weights(hd, slot, qt, diagonal):
        run = None if diagonal else later_ref[qt, hd]
        for c in range(SB_NB):
            arg = pre_ref[slot, hd, :, c * SB_TK:(c + 1) * SB_TK]
            w = jnp.exp2(arg if run is None else arg + run)
            if diagonal:
                w = jnp.where(rel[c] < 0, w, 0.0)
            lo = hd * SB_SPAN + c * SB_TK
            w_ref[slot, :, lo:lo + SB_TK] = w.astype(BF16)
        tot = tot_ref[slot, hd]
        later_ref[qt, hd] = tot if run is None else run + tot

    def sweep(sq_ref, sk_ref, n_steps, diagonal):
        def step(s):
            return sq_ref[s + 1], sk_ref[s + 1]

        def weighted_values(s, slot):
            qt, kt = step(s)
            pv = jnp.dot(w_ref[slot], vh_ref[kt], preferred_element_type=F32)
            return qt, pv

        def accumulate(qt, pv):
            if diagonal:
                acc_ref[qt] = pv
            else:
                acc_ref[qt] += pv

        def half(s, slot):
            t = [t_ref[1 - slot, hd] for hd in range(2)]
            t_new = [scores(0, *step(s + 2))]
            sums = [suffix_sums(t[0], diagonal)]
            qt_prev, pv = weighted_values(s - 1, 1 - slot)
            t_new.append(scores(1, *step(s + 2)))
            sums.append(suffix_sums(t[1], diagonal))
            accumulate(qt_prev, pv)
            qt, _ = step(s)
            for hd in range(2):
                weights(hd, slot, qt, diagonal)
            for hd in range(2):
                t_ref[slot, hd] = t_new[hd]
            for hd in range(2):
                stage_sums(t[hd], sums[hd], hd, 1 - slot)

        for hd in range(2):
            t0 = scores(hd, *step(0))
            stage_sums(t0, suffix_sums(t0, diagonal), hd, 0)
        for hd in range(2):
            t_ref[1, hd] = scores(hd, *step(1))

        def body(i, carry):
            for u in range(SB_UNROLL):
                half(SB_UNROLL * i + u, u % 2)
            return carry

        lax.fori_loop(0, n_steps // SB_UNROLL, body, 0)
        accumulate(*weighted_values(n_steps - 1, 1))

    sweep(dq_ref, dk_ref, n_tiles, True)
    sweep(eq_ref, ek_ref, n_tiles * (n_tiles - 1) // 2, False)
    for tile in range(n_tiles):
        o_ref[0, tile * SB_TQ:(tile + 1) * SB_TQ, :] = acc_ref[tile].astype(o_ref.dtype)


def _sb_steps(n_tiles):
    diag = [(t, t) for t in range(n_tiles)]
    early = [(t, kt) for t in range(1, n_tiles) for kt in range(t - 1, -1, -1)]
    assert len(diag) % SB_UNROLL == 0 and len(early) % SB_UNROLL == 0
    tables = []
    for steps in (diag, early):
        steps = [(n_tiles, 0)] + steps + [steps[-1]] * 2
        tables += [jnp.asarray([s[0] for s in steps], jnp.int32),
                   jnp.asarray([s[1] for s in steps], jnp.int32)]
    return tables


def _sb_attention(q, k, v, tri):
    b, seq, _ = q.shape
    pairs = SB_WIDTH // LANES
    n_tiles = seq // SB_TQ
    spec = pl.BlockSpec((1, seq, LANES), lambda bi, p: (bi, 0, p))
    smem = pl.BlockSpec(memory_space=pltpu.SMEM)
    return pl.pallas_call(
        _sb_kernel,
        grid=(b, pairs),
        in_specs=[smem, smem, smem, smem, spec, spec, spec,
                  pl.BlockSpec((SB_SPAN, SB_SPAN), lambda bi, p: (0, 0))],
        out_specs=spec,
        out_shape=jax.ShapeDtypeStruct((b, seq, SB_WIDTH), BF16),
        scratch_shapes=[pltpu.VMEM((2, seq, LANES), BF16),
                        pltpu.VMEM((n_tiles, 2 * SB_SPAN, LANES), BF16),
                        pltpu.VMEM((2, 2, SB_TQ, SB_SPAN), F32),
                        pltpu.VMEM((2, 2, SB_TQ, SB_SPAN), F32),
                        pltpu.VMEM((2, 2, SB_TQ, SB_TK), F32),
                        pltpu.VMEM((2, SB_TQ, 2 * SB_SPAN), BF16),
                        pltpu.VMEM((n_tiles, 2, SB_TQ, SB_TK), F32),
                        pltpu.VMEM((n_tiles + 1, SB_TQ, LANES), F32)],
        compiler_params=pltpu.CompilerParams(
            dimension_semantics=("arbitrary", "arbitrary"),
            vmem_limit_bytes=VMEM_LIMIT),
        name="sb_attn",
    )(*_sb_steps(n_tiles), q, k, v, tri)


def _ret_kernel(lg_ref, q_ref, k_ref, v_ref, g_ref, o_ref,
                decay_ref, qdec_ref, kdec_ref, state_ref):
    seq = q_ref.shape[1]
    c = RET_CHUNK
    heads = range(RET_HEADS)
    diff = (lax.broadcasted_iota(jnp.int32, (c, c), 0)
            - lax.broadcasted_iota(jnp.int32, (c, c), 1)).astype(F32)
    idx = lax.broadcasted_iota(jnp.int32, (c, LANES), 0).astype(F32)
    for h in heads:
        lg = lg_ref[h]
        decay_ref[h] = jnp.where(diff >= 0, jnp.exp(jnp.maximum(diff, 0.0) * lg), 0.0)
        qdec_ref[h] = jnp.exp((idx + 1.0) * lg)
        kdec_ref[h] = jnp.exp((c - 1.0 - idx) * lg)
    state_ref[...] = jnp.zeros(state_ref.shape, F32)

    def body(n, carry):
        sl = pl.ds(pl.multiple_of(n * c, c), c)
        q, k, v = ([ref[0, sl, h * LANES:(h + 1) * LANES] for h in heads]
                   for ref in (q_ref, k_ref, v_ref))
        state = [state_ref[h] for h in heads]
        scores = [lax.dot_general(q[h], k[h], (((1,), (1,)), ((), ())),
                                  preferred_element_type=F32) for h in heads]
        cross = [jnp.dot((q[h].astype(F32) * qdec_ref[h]).astype(BF16),
                         state[h].astype(BF16), preferred_element_type=F32)
                 for h in heads]
        kd_t = [(k[h].astype(F32) * kdec_ref[h]).T.astype(BF16) for h in heads]
        kv = [jnp.dot(kd_t[h], v[h], preferred_element_type=F32) for h in heads]
        inner = [jnp.dot((scores[h] * decay_ref[h]).astype(BF16), v[h],
                         preferred_element_type=F32) for h in heads]
        for h in heads:
            chunk_decay = jnp.exp(jnp.full((1, LANES), float(c), F32) * lg_ref[h])
            state_ref[h] = chunk_decay * state[h] + kv[h]
        for h in heads:
            o = inner[h] + cross[h]
            mu = jnp.mean(o, axis=-1, keepdims=True)
            ctr = o - mu
            var = jnp.mean(ctr * ctr, axis=-1, keepdims=True)
            gate = g_ref[0, sl, h * LANES:(h + 1) * LANES]
            o_ref[0, sl, h * LANES:(h + 1) * LANES] = (
                ctr * lax.rsqrt(var + GN_EPS)
                * (gate * jax.nn.sigmoid(gate))).astype(o_ref.dtype)
        return carry

    lax.fori_loop(0, seq // c, body, 0)


def _retention(log_gamma, q, k, v, g):
    b, seq, width = q.shape
    spec = pl.BlockSpec((1, seq, width), lambda bi: (bi, 0, 0))
    c = RET_CHUNK
    return pl.pallas_call(
        _ret_kernel,
        grid=(b,),
        in_specs=[pl.BlockSpec(memory_space=pltpu.SMEM), spec, spec, spec, spec],
        out_specs=spec,
        out_shape=jax.ShapeDtypeStruct((b, seq, RET_WIDTH), BF16),
        scratch_shapes=[pltpu.VMEM((RET_HEADS, c, c), F32),
                        pltpu.VMEM((RET_HEADS, c, LANES), F32),
                        pltpu.VMEM((RET_HEADS, c, LANES), F32),
                        pltpu.VMEM((RET_HEADS, RET_KEY_DIM, LANES), F32)],
        compiler_params=pltpu.CompilerParams(
            dimension_semantics=("arbitrary",), vmem_limit_bytes=VMEM_LIMIT),
        name="retention",
    )(log_gamma, q, k, v, g)


def _merge_kernel(x_ref, osb_ref, or_ref, gpre_ref, wgate_ref,
                  wsb_ref, wret_ref, wout_ref, gpost_ref, o_ref):
    x = x_ref[...]
    h = _rms(x, gpre_ref[...]).astype(BF16)
    gate_sb = jnp.dot(h, wgate_ref[:, :D_MODEL], preferred_element_type=F32)
    p_sb = jnp.dot(osb_ref[...], wsb_ref[...], preferred_element_type=F32)
    merged = jax.nn.sigmoid(gate_sb) * p_sb
    gate_ret = jnp.dot(h, wgate_ref[:, D_MODEL:], preferred_element_type=F32)
    p_ret = jnp.dot(or_ref[...], wret_ref[...], preferred_element_type=F32)
    merged = merged + jax.nn.sigmoid(gate_ret) * p_ret
    y = jnp.dot(merged.astype(BF16), wout_ref[...], preferred_element_type=F32)
    o_ref[...] = x + _rms(y, gpost_ref[...])


def _merge(x2d, o_sb, o_r, g_pre, w_gates, w_sb, w_ret, w_out, g_post):
    m = x2d.shape[0]

    def tile(width):
        return pl.BlockSpec((MIX_TM, width), lambda i: (i, 0))

    gain = pl.BlockSpec((1, D_MODEL), lambda i: (0, 0))
    return pl.pallas_call(
        _merge_kernel,
        grid=(m // MIX_TM,),
        in_specs=[tile(D_MODEL), tile(SB_WIDTH), tile(RET_WIDTH),
                  gain, _resident((D_MODEL, 2 * D_MODEL)),
                  _resident((SB_WIDTH, D_MODEL)), _resident((RET_WIDTH, D_MODEL)),
                  _resident((D_MODEL, D_MODEL)), gain],
        out_specs=tile(D_MODEL),
        out_shape=jax.ShapeDtypeStruct((m, D_MODEL), F32),
        compiler_params=pltpu.CompilerParams(
            dimension_semantics=("arbitrary",), vmem_limit_bytes=VMEM_LIMIT),
        name="merge",
    )(x2d, o_sb, o_r, g_pre, w_gates, w_sb, w_ret, w_out, g_post)


def _rope_tables(seq):
    half = RET_KEY_DIM // 2
    pos = jnp.arange(seq, dtype=F32)
    inv_freq = ROPE_BASE ** (-jnp.arange(half, dtype=F32) / half)
    ang = pos[:, None] * inv_freq[None, :]
    cos, sin = jnp.cos(ang), jnp.sin(ang)
    return (jnp.concatenate([cos, cos], axis=-1),
            jnp.concatenate([-sin, sin], axis=-1))


def _suffix_sum_matrix():
    j = np.arange(SB_SPAN)[:, None]
    s = np.arange(SB_SPAN)[None, :]
    return jnp.asarray(-(j >= s).astype(np.float32), dtype=BF16)


def kernel(x, g_ffn1_pre, g_ffn1_post, w_ffn1_gate, w_ffn1_up, w_ffn1_down,
           g_mix_pre, w_in, w_proj_sb, w_proj_ret, w_out, g_mix_post,
           g_ffn2_pre, g_ffn2_post, w_ffn2_gate, w_ffn2_up, w_ffn2_down):
    b, seq, d = x.shape
    depth = w_in.shape[0]
    log_gamma = jnp.log(1.0 - 2.0 ** (-5.0 - jnp.arange(RET_HEADS, dtype=F32)))
    cosf, sinf = _rope_tables(seq)
    tri = _suffix_sum_matrix()
    x2d = x.reshape(b * seq, d)
    for l in range(depth):
        x2d = _ffn(x2d, g_ffn1_pre[l][None], g_ffn1_post[l][None],
                   w_ffn1_gate[l].astype(BF16), w_ffn1_up[l].astype(BF16),
                   w_ffn1_down[l].astype(BF16))
        w_in_l = w_in[l].astype(BF16)
        n_gate = 2 * D_MODEL
        q_sb, k_sb, v_sb, q_r, k_r, v_r, g_r = _mix_in(
            x2d, g_mix_pre[l][None], w_in_l[:, :-n_gate], cosf, sinf, seq)
        shp = (b, seq, SB_WIDTH)
        o_sb = _sb_attention(q_sb.reshape(shp), k_sb.reshape(shp),
                             v_sb.reshape(shp), tri)
        o_r = _retention(log_gamma, q_r.reshape(shp), k_r.reshape(shp),
                         v_r.reshape(shp), g_r.reshape(shp))
        x2d = _merge(x2d, o_sb.reshape(b * seq, SB_WIDTH),
                     o_r.reshape(b * seq, RET_WIDTH),
                     g_mix_pre[l][None], w_in_l[:, -n_gate:],
                     w_proj_sb[l].astype(BF16), w_proj_ret[l].astype(BF16),
                     w_out[l].astype(BF16), g_mix_post[l][None])
        x2d = _ffn(x2d, g_ffn2_pre[l][None], g_ffn2_post[l][None],
                   w_ffn2_gate[l].astype(BF16), w_ffn2_up[l].astype(BF16),
                   w_ffn2_down[l].astype(BF16))
    return x2d.reshape(b, seq, d)
```

```python
import functools
import math

import numpy as np
import jax
import jax.numpy as jnp
from jax import lax
from jax.experimental import pallas as pl
from jax.experimental.pallas import tpu as pltpu

F32 = jnp.float32
BF16 = jnp.bfloat16

D_MODEL = 1024
D_FF = 4 * D_MODEL
SB_HEADS = 8
SB_HEAD_DIM = 64
SB_WIDTH = SB_HEADS * SB_HEAD_DIM
RET_HEADS = 4
RET_KEY_DIM = 128
RET_WIDTH = RET_HEADS * RET_KEY_DIM
RMS_EPS = 1e-6
GN_EPS = 1e-5
ROPE_BASE = 10000.0

LANES = 128
BF16_SUBLANES = 16
VMEM_LIMIT = 56 * 1024 * 1024

FFN_TM = 512
FFN_TF = 1024
MIX_TM = 512
SB_TQ = 256
SB_TK = LANES
SB_SPAN = SB_TQ
SB_NB = SB_SPAN // SB_TK
SB_MAX_UNROLL = 14
SB_Q_SCALE = SB_HEAD_DIM ** -0.5 * math.log2(math.e)
RET_CHUNK = 256


def _rms(x, g):
    ms = jnp.mean(x * x, axis=-1, keepdims=True)
    return x * lax.rsqrt(ms + RMS_EPS) * g


def _resident(shape):
    nd = len(shape)
    return pl.BlockSpec(shape, lambda *_: (0,) * nd, pipeline_mode=pl.Buffered(1))


def _ffn_kernel(n_casts, x_ref, gpre_ref, gpost_ref, wg_ref, wu_ref, wd_ref, *refs):
    cast_in, o_ref, cast_out = refs[:n_casts], refs[n_casts], refs[n_casts + 1:]
    for src, dst in zip(cast_in, cast_out):
        dst[...] = src[...].astype(dst.dtype)
    x = x_ref[...]
    h = _rms(x, gpre_ref[...]).astype(BF16)
    acc = None
    for j in range(D_FF // FFN_TF):
        cols = slice(j * FFN_TF, (j + 1) * FFN_TF)
        g = jnp.dot(h, wg_ref[:, cols], preferred_element_type=F32)
        u = jnp.dot(h, wu_ref[:, cols], preferred_element_type=F32)
        a = (g * jax.nn.sigmoid(g) * u).astype(BF16)
        d = jnp.dot(a, wd_ref[cols, :], preferred_element_type=F32)
        acc = d if acc is None else acc + d
    o_ref[...] = x + 0.5 * _rms(acc, gpost_ref[...])


def _ffn(x2d, g_pre, g_post, wg, wu, wd, to_bf16=()):
    m = x2d.shape[0]
    steps = m // FFN_TM
    tile = pl.BlockSpec((FFN_TM, D_MODEL), lambda i: (i, 0))
    gain = pl.BlockSpec((1, D_MODEL), lambda i: (0, 0))
    slabs = [pl.BlockSpec((w.shape[0] // steps, w.shape[1]), lambda i: (i, 0))
             for w in to_bf16]
    assert all(w.shape[0] % (steps * BF16_SUBLANES) == 0 for w in to_bf16)
    outs = pl.pallas_call(
        functools.partial(_ffn_kernel, len(to_bf16)),
        grid=(steps,),
        in_specs=[tile, gain, gain,
                  _resident((D_MODEL, D_FF)), _resident((D_MODEL, D_FF)),
                  _resident((D_FF, D_MODEL))] + slabs,
        out_specs=[tile] + slabs,
        out_shape=[jax.ShapeDtypeStruct((m, D_MODEL), F32)]
                  + [jax.ShapeDtypeStruct(w.shape, BF16) for w in to_bf16],
        compiler_params=pltpu.CompilerParams(
            dimension_semantics=("arbitrary",), vmem_limit_bytes=VMEM_LIMIT),
        name="ffn",
    )(x2d, g_pre, g_post, wg, wu, wd, *to_bf16)
    return outs[0], outs[1:]


def _mix_in_kernel(x_ref, g_ref, w_ref, cos_ref, sin_ref,
                   qsb_ref, ksb_ref, vsb_ref, qr_ref, kr_ref, vr_ref, gr_ref):
    h = _rms(x_ref[...], g_ref[...]).astype(BF16)

    def mm(lo, width):
        return jnp.dot(h, w_ref[:, lo:lo + width], preferred_element_type=F32)

    def rotary(t):
        cosf = cos_ref[...]
        sinf = sin_ref[...]
        parts = []
        for hd in range(RET_HEADS):
            th = t[:, hd * LANES:(hd + 1) * LANES]
            parts.append(th * cosf + pltpu.roll(th, LANES // 2, axis=1) * sinf)
        return jnp.concatenate(parts, axis=1)

    c = 0
    qsb_ref[...] = (mm(c, SB_WIDTH) * SB_Q_SCALE).astype(BF16); c += SB_WIDTH
    ksb_ref[...] = mm(c, SB_WIDTH).astype(BF16); c += SB_WIDTH
    vsb_ref[...] = mm(c, SB_WIDTH).astype(BF16); c += SB_WIDTH
    qr_ref[...] = rotary(mm(c, RET_WIDTH)).astype(BF16); c += RET_WIDTH
    kr_ref[...] = (rotary(mm(c, RET_WIDTH)) * (RET_KEY_DIM ** -0.5)).astype(BF16); c += RET_WIDTH
    vr_ref[...] = mm(c, RET_WIDTH).astype(BF16); c += RET_WIDTH
    gr_ref[...] = mm(c, RET_WIDTH)


def _mix_in(x2d, g, w_in, cosf, sinf, seq):
    m = x2d.shape[0]
    n_in = w_in.shape[1]
    pos_blocks = seq // MIX_TM

    def tile(width):
        return pl.BlockSpec((MIX_TM, width), lambda i: (i, 0))

    rope = pl.BlockSpec((MIX_TM, LANES), lambda i: (i % pos_blocks, 0))
    widths = [SB_WIDTH] * 3 + [RET_WIDTH] * 4
    dtypes = [BF16] * 6 + [F32]
    return pl.pallas_call(
        _mix_in_kernel,
        grid=(m // MIX_TM,),
        in_specs=[tile(D_MODEL), pl.BlockSpec((1, D_MODEL), lambda i: (0, 0)),
                  _resident((D_MODEL, n_in)), rope, rope],
        out_specs=[tile(w) for w in widths],
        out_shape=[jax.ShapeDtypeStruct((m, w), dt) for w, dt in zip(widths, dtypes)],
        compiler_params=pltpu.CompilerParams(
            dimension_semantics=("arbitrary",), vmem_limit_bytes=VMEM_LIMIT),
        name="mix_in",
    )(x2d, g, w_in, cosf, sinf)


def _sb_kernel(dq_ref, dk_ref, eq_ref, ek_ref, q_ref, k_ref, v_ref, tri_ref, o_ref,
               qh_ref, kh_ref, vh_ref, t_ref, pre_ref, tot_ref, w_ref, later_ref, acc_ref):
    seq = q_ref.shape[1]
    pairs = q_ref.shape[2] // LANES
    seq_tiles = seq // SB_TQ
    n_tiles = pairs * seq_tiles
    lane = lax.broadcasted_iota(jnp.int32, (SB_TQ, LANES), 1)
    head0 = lane < SB_HEAD_DIM
    row = lax.broadcasted_iota(jnp.int32, (SB_TQ, SB_TK), 0)
    col = lax.broadcasted_iota(jnp.int32, (SB_TQ, SB_TK), 1)
    rel = [col - row + c * SB_TK for c in range(SB_NB)]
    tri = tri_ref[...]

    def tile_rows(tile):
        return pl.ds(pl.multiple_of(tile * SB_TQ, SB_TQ), SB_TQ)

    for tile in range(n_tiles):
        src = (0, slice((tile % seq_tiles) * SB_TQ, (tile % seq_tiles + 1) * SB_TQ),
               slice((tile // seq_tiles) * LANES, (tile // seq_tiles + 1) * LANES))
        dst = slice(tile * SB_TQ, (tile + 1) * SB_TQ)
        q = q_ref[src]
        v = v_ref[src]
        zero = jnp.zeros_like(q)
        qh_ref[0, dst, :] = jnp.where(head0, q, zero)
        qh_ref[1, dst, :] = jnp.where(head0, zero, q)
        kh_ref[dst, :] = k_ref[src]
        vh_ref[tile, :SB_SPAN, :] = jnp.where(head0, v, zero)
        vh_ref[tile, SB_SPAN:, :] = jnp.where(head0, zero, v)
    w_ref[1] = jnp.zeros(w_ref.shape[1:], w_ref.dtype)

    def scores(hd, qt, kt):
        return lax.dot_general(qh_ref[hd, tile_rows(qt), :], kh_ref[tile_rows(kt), :],
                               (((1,), (1,)), ((), ())), preferred_element_type=F32)

    def suffix_sums(t, diagonal):
        sp = jnp.maximum(t, 0.0) + jnp.log2(1.0 + jnp.exp2(-jnp.abs(t)))
        if diagonal:
            sp = jnp.concatenate(
                [jnp.where(rel[c] < 0, sp[:, c * SB_TK:(c + 1) * SB_TK], 0.0)
                 for c in range(SB_NB)], axis=1)
        return jnp.dot(sp.astype(BF16), tri, preferred_element_type=F32)

    def stage_sums(t, sums, hd, slot):
        pre_ref[slot, hd] = t + sums
        tot_ref[slot, hd] = jnp.broadcast_to(sums[:, 0:1], (SB_TQ, SB_TK))

    def weights(hd, slot, qt, diagonal):
        run = None if diagonal else later_ref[qt, hd]
        for c in range(SB_NB):
            arg = pre_ref[slot, hd, :, c * SB_TK:(c + 1) * SB_TK]
            w = jnp.exp2(arg if run is None else arg + run)
            if diagonal:
                w = jnp.where(rel[c] < 0, w, 0.0)
            lo = hd * SB_SPAN + c * SB_TK
            w_ref[slot, :, lo:lo + SB_TK] = w.astype(BF16)
        tot = tot_ref[slot, hd]
        later_ref[qt, hd] = tot if run is None else run + tot

    def sweep(sq_ref, sk_ref, n_steps, diagonal):
        def step(s):
            return sq_ref[s + 1], sk_ref[s + 1]

        def weighted_values(s, slot):
            qt, kt = step(s)
            pv = jnp.dot(w_ref[slot], vh_ref[kt], preferred_element_type=F32)
            return qt, pv

        def accumulate(qt, pv):
            if diagonal:
                acc_ref[qt] = pv
            else:
                acc_ref[qt] += pv

        def half(s, slot):
            t = [t_ref[1 - slot, hd] for hd in range(2)]
            t_new = [scores(0, *step(s + 2))]
            sums = [suffix_sums(t[0], diagonal)]
            qt_prev, pv = weighted_values(s - 1, 1 - slot)
            t_new.append(scores(1, *step(s + 2)))
            sums.append(suffix_sums(t[1], diagonal))
            accumulate(qt_prev, pv)
            qt, _ = step(s)
            for hd in range(2):
                weights(hd, slot, qt, diagonal)
            for hd in range(2):
                t_ref[slot, hd] = t_new[hd]
            for hd in range(2):
                stage_sums(t[hd], sums[hd], hd, 1 - slot)

        for hd in range(2):
            t0 = scores(hd, *step(0))
            stage_sums(t0, suffix_sums(t0, diagonal), hd, 0)
        for hd in range(2):
            t_ref[1, hd] = scores(hd, *step(1))

        unroll = _sb_unroll(n_steps)

        def body(i, carry):
            for u in range(unroll):
                half(unroll * i + u, u % 2)
            return carry

        lax.fori_loop(0, n_steps // unroll, body, 0)
        accumulate(*weighted_values(n_steps - 1, 1))

    sweep(dq_ref, dk_ref, n_tiles, True)
    sweep(eq_ref, ek_ref, pairs * seq_tiles * (seq_tiles - 1) // 2, False)
    for tile in range(n_tiles):
        o_ref[0, (tile % seq_tiles) * SB_TQ:(tile % seq_tiles + 1) * SB_TQ,
              (tile // seq_tiles) * LANES:(tile // seq_tiles + 1) * LANES] = (
                  acc_ref[tile].astype(o_ref.dtype))


def _sb_unroll(n_steps):
    return max(u for u in range(2, SB_MAX_UNROLL + 1, 2) if n_steps % u == 0)


def _sb_steps(pairs, seq_tiles):
    n_tiles = pairs * seq_tiles
    diag = [(g, g) for g in range(n_tiles)]
    early = [(p * seq_tiles + t, p * seq_tiles + kt) for p in range(pairs)
             for t in range(1, seq_tiles) for kt in range(t - 1, -1, -1)]
    tables = []
    for steps in (diag, early):
        steps = [(n_tiles, 0)] + steps + [steps[-1]] * 2
        tables += [jnp.asarray([s[0] for s in steps], jnp.int32),
                   jnp.asarray([s[1] for s in steps], jnp.int32)]
    return tables


def _sb_attention(q, k, v, tri):
    b, seq, width = q.shape
    pairs = width // LANES
    seq_tiles = seq // SB_TQ
    n_tiles = pairs * seq_tiles
    spec = pl.BlockSpec((1, seq, width), lambda bi: (bi, 0, 0))
    smem = pl.BlockSpec(memory_space=pltpu.SMEM)
    return pl.pallas_call(
        _sb_kernel,
        grid=(b,),
        in_specs=[smem, smem, smem, smem, spec, spec, spec,
                  pl.BlockSpec((SB_SPAN, SB_SPAN), lambda bi: (0, 0))],
        out_specs=spec,
        out_shape=jax.ShapeDtypeStruct((b, seq, width), BF16),
        scratch_shapes=[pltpu.VMEM((2, n_tiles * SB_TQ, LANES), BF16),
                        pltpu.VMEM((n_tiles * SB_TQ, LANES), BF16),
                        pltpu.VMEM((n_tiles, 2 * SB_SPAN, LANES), BF16),
                        pltpu.VMEM((2, 2, SB_TQ, SB_SPAN), F32),
                        pltpu.VMEM((2, 2, SB_TQ, SB_SPAN), F32),
                        pltpu.VMEM((2, 2, SB_TQ, SB_TK), F32),
                        pltpu.VMEM((2, SB_TQ, 2 * SB_SPAN), BF16),
                        pltpu.VMEM((n_tiles, 2, SB_TQ, SB_TK), F32),
                        pltpu.VMEM((n_tiles + 1, SB_TQ, LANES), F32)],
        compiler_params=pltpu.CompilerParams(
            dimension_semantics=("arbitrary",), vmem_limit_bytes=VMEM_LIMIT),
        name="sb_attn",
    )(*_sb_steps(pairs, seq_tiles), q, k, v, tri)


def _ret_kernel(lg_ref, q_ref, k_ref, v_ref, g_ref, o_ref,
                decay_ref, qdec_ref, kdec_ref, state_ref):
    seq = q_ref.shape[1]
    c = RET_CHUNK
    heads = range(RET_HEADS)
    diff = (lax.broadcasted_iota(jnp.int32, (c, c), 0)
            - lax.broadcasted_iota(jnp.int32, (c, c), 1)).astype(F32)
    idx = lax.broadcasted_iota(jnp.int32, (c, LANES), 0).astype(F32)
    for h in heads:
        lg = lg_ref[h]
        decay_ref[h] = jnp.where(diff >= 0, jnp.exp(jnp.maximum(diff, 0.0) * lg), 0.0)
        qdec_ref[h] = jnp.exp((idx + 1.0) * lg)
        kdec_ref[h] = jnp.exp((c - 1.0 - idx) * lg)
    state_ref[...] = jnp.zeros(state_ref.shape, F32)

    def body(n, carry):
        sl = pl.ds(pl.multiple_of(n * c, c), c)
        q, k, v = ([ref[0, sl, h * LANES:(h + 1) * LANES] for h in heads]
                   for ref in (q_ref, k_ref, v_ref))
        state = [state_ref[h] for h in heads]
        scores = [lax.dot_general(q[h], k[h], (((1,), (1,)), ((), ())),
                                  preferred_element_type=F32) for h in heads]
        cross = [jnp.dot((q[h].astype(F32) * qdec_ref[h]).astype(BF16),
                         state[h].astype(BF16), preferred_element_type=F32)
                 for h in heads]
        kd_t = [(k[h].astype(F32) * kdec_ref[h]).T.astype(BF16) for h in heads]
        kv = [jnp.dot(kd_t[h], v[h], preferred_element_type=F32) for h in heads]
        inner = [jnp.dot((scores[h] * decay_ref[h]).astype(BF16), v[h],
                         preferred_element_type=F32) for h in heads]
        for h in heads:
            chunk_decay = jnp.exp(jnp.full((1, LANES), float(c), F32) * lg_ref[h])
            state_ref[h] = chunk_decay * state[h] + kv[h]
        for h in heads:
            o = inner[h] + cross[h]
            mu = jnp.mean(o, axis=-1, keepdims=True)
            ctr = o - mu
            var = jnp.mean(ctr * ctr, axis=-1, keepdims=True)
            gate = g_ref[0, sl, h * LANES:(h + 1) * LANES]
            o_ref[0, sl, h * LANES:(h + 1) * LANES] = (
                ctr * lax.rsqrt(var + GN_EPS)
                * (gate * jax.nn.sigmoid(gate))).astype(o_ref.dtype)
        return carry

    lax.fori_loop(0, seq // c, body, 0)


def _retention(log_gamma, q, k, v, g):
    b, seq, width = q.shape
    spec = pl.BlockSpec((1, seq, width), lambda bi: (bi, 0, 0))
    c = RET_CHUNK
    return pl.pallas_call(
        _ret_kernel,
        grid=(b,),
        in_specs=[pl.BlockSpec(memory_space=pltpu.SMEM), spec, spec, spec, spec],
        out_specs=spec,
        out_shape=jax.ShapeDtypeStruct((b, seq, RET_WIDTH), BF16),
        scratch_shapes=[pltpu.VMEM((RET_HEADS, c, c), F32),
                        pltpu.VMEM((RET_HEADS, c, LANES), F32),
                        pltpu.VMEM((RET_HEADS, c, LANES), F32),
                        pltpu.VMEM((RET_HEADS, RET_KEY_DIM, LANES), F32)],
        compiler_params=pltpu.CompilerParams(
            dimension_semantics=("arbitrary",), vmem_limit_bytes=VMEM_LIMIT),
        name="retention",
    )(log_gamma, q, k, v, g)


def _merge_kernel(x_ref, osb_ref, or_ref, gpre_ref, wgate_ref,
                  wsb_ref, wret_ref, wout_ref, gpost_ref, o_ref):
    x = x_ref[...]
    h = _rms(x, gpre_ref[...]).astype(BF16)
    lo = wgate_ref.shape[1] - 2 * D_MODEL
    gate_sb = jnp.dot(h, wgate_ref[:, lo:lo + D_MODEL], preferred_element_type=F32)
    p_sb = jnp.dot(osb_ref[...], wsb_ref[...], preferred_element_type=F32)
    merged = jax.nn.sigmoid(gate_sb) * p_sb
    gate_ret = jnp.dot(h, wgate_ref[:, lo + D_MODEL:], preferred_element_type=F32)
    p_ret = jnp.dot(or_ref[...], wret_ref[...], preferred_element_type=F32)
    merged = merged + jax.nn.sigmoid(gate_ret) * p_ret
    y = jnp.dot(merged.astype(BF16), wout_ref[...], preferred_element_type=F32)
    o_ref[...] = x + _rms(y, gpost_ref[...])


def _merge(x2d, o_sb, o_r, g_pre, w_gates, w_sb, w_ret, w_out, g_post):
    m = x2d.shape[0]

    def tile(width):
        return pl.BlockSpec((MIX_TM, width), lambda i: (i, 0))

    gain = pl.BlockSpec((1, D_MODEL), lambda i: (0, 0))
    return pl.pallas_call(
        _merge_kernel,
        grid=(m // MIX_TM,),
        in_specs=[tile(D_MODEL), tile(SB_WIDTH), tile(RET_WIDTH),
                  gain, _resident(w_gates.shape),
                  _resident((SB_WIDTH, D_MODEL)), _resident((RET_WIDTH, D_MODEL)),
                  _resident((D_MODEL, D_MODEL)), gain],
        out_specs=tile(D_MODEL),
        out_shape=jax.ShapeDtypeStruct((m, D_MODEL), F32),
        compiler_params=pltpu.CompilerParams(
            dimension_semantics=("arbitrary",), vmem_limit_bytes=VMEM_LIMIT),
        name="merge",
    )(x2d, o_sb, o_r, g_pre, w_gates, w_sb, w_ret, w_out, g_post)


def _rope_tables(seq):
    half = RET_KEY_DIM // 2
    pos = jnp.arange(seq, dtype=F32)
    inv_freq = ROPE_BASE ** (-jnp.arange(half, dtype=F32) / half)
    ang = pos[:, None] * inv_freq[None, :]
    cos, sin = jnp.cos(ang), jnp.sin(ang)
    return (jnp.concatenate([cos, cos], axis=-1),
            jnp.concatenate([-sin, sin], axis=-1))


def _suffix_sum_matrix():
    j = np.arange(SB_SPAN)[:, None]
    s = np.arange(SB_SPAN)[None, :]
    return jnp.asarray(-(j >= s).astype(np.float32), dtype=BF16)


def kernel(x, g_ffn1_pre, g_ffn1_post, w_ffn1_gate, w_ffn1_up, w_ffn1_down,
           g_mix_pre, w_in, w_proj_sb, w_proj_ret, w_out, g_mix_post,
           g_ffn2_pre, g_ffn2_post, w_ffn2_gate, w_ffn2_up, w_ffn2_down):
    b, seq, d = x.shape
    depth = w_in.shape[0]
    log_gamma = jnp.log(1.0 - 2.0 ** (-5.0 - jnp.arange(RET_HEADS, dtype=F32)))
    cosf, sinf = _rope_tables(seq)
    tri = _suffix_sum_matrix()
    x2d = x.reshape(b * seq, d)
    for l in range(depth):
        x2d, (w_in_l, w_sb_l, w_ret_l, w_out_l, w2_gate, w2_up, w2_down) = _ffn(
            x2d, g_ffn1_pre[l][None], g_ffn1_post[l][None],
            w_ffn1_gate[l].astype(BF16), w_ffn1_up[l].astype(BF16),
            w_ffn1_down[l].astype(BF16),
            to_bf16=(w_in[l], w_proj_sb[l], w_proj_ret[l], w_out[l],
                     w_ffn2_gate[l], w_ffn2_up[l], w_ffn2_down[l]))
        q_sb, k_sb, v_sb, q_r, k_r, v_r, g_r = _mix_in(
            x2d, g_mix_pre[l][None], w_in_l, cosf, sinf, seq)
        shp = (b, seq, SB_WIDTH)
        o_sb = _sb_attention(q_sb.reshape(shp), k_sb.reshape(shp),
                             v_sb.reshape(shp), tri)
        o_r = _retention(log_gamma, q_r.reshape(shp), k_r.reshape(shp),
                         v_r.reshape(shp), g_r.reshape(shp))
        x2d = _merge(x2d, o_sb.reshape(b * seq, SB_WIDTH),
                     o_r.reshape(b * seq, RET_WIDTH),
                     g_mix_pre[l][None], w_in_l, w_sb_l, w_ret_l, w_out_l,
                     g_mix_post[l][None])
        x2d, _ = _ffn(x2d, g_ffn2_pre[l][None], g_ffn2_post[l][None],
                      w2_gate, w2_up, w2_down)
    return x2d.reshape(b, seq, d)
```

```python
import functools
import math

import numpy as np
import jax
import jax.numpy as jnp
from jax import lax
from jax.experimental import pallas as pl
from jax.experimental.pallas import tpu as pltpu

F32 = jnp.float32
BF16 = jnp.bfloat16

D_MODEL = 1024
D_FF = 4 * D_MODEL
SB_HEADS = 8
SB_HEAD_DIM = 64
SB_WIDTH = SB_HEADS * SB_HEAD_DIM
RET_HEADS = 4
RET_KEY_DIM = 128
RET_WIDTH = RET_HEADS * RET_KEY_DIM
RMS_EPS = 1e-6
GN_EPS = 1e-5
ROPE_BASE = 10000.0

LANES = 128
BF16_SUBLANES = 16
VMEM_LIMIT = 56 * 1024 * 1024

FFN_TM = 512
FFN_TF = 1024
FFN_ROW_GROUPS = 2
MIX_TM = 512
MIX_ROW_GROUPS = 2
SB_TQ = 256
SB_TK = LANES
SB_SPAN = SB_TQ
SB_NB = SB_SPAN // SB_TK
SB_MAX_UNROLL = 14
SB_Q_SCALE = SB_HEAD_DIM ** -0.5 * math.log2(math.e)
RET_CHUNK = 256


def _rms(x, g):
    ms = jnp.mean(x * x, axis=-1, keepdims=True)
    return x * lax.rsqrt(ms + RMS_EPS) * g


def _resident(shape):
    nd = len(shape)
    return pl.BlockSpec(shape, lambda *_: (0,) * nd, pipeline_mode=pl.Buffered(1))


def _ffn_kernel(n_casts, x_ref, gpre_ref, gpost_ref, wg_ref, wu_ref, wd_ref, *refs):
    cast_in, o_ref, cast_out = refs[:n_casts], refs[n_casts], refs[n_casts + 1:]
    for src, dst in zip(cast_in, cast_out):
        dst[...] = src[...].astype(dst.dtype)
    rows = [slice(r * FFN_TM // FFN_ROW_GROUPS, (r + 1) * FFN_TM // FFN_ROW_GROUPS)
            for r in range(FFN_ROW_GROUPS)]
    x = [x_ref[r, :] for r in rows]
    h = [_rms(xr, gpre_ref[...]).astype(BF16) for xr in x]
    acc = [None] * FFN_ROW_GROUPS
    for j in range(D_FF // FFN_TF):
        cols = slice(j * FFN_TF, (j + 1) * FFN_TF)
        for r in range(FFN_ROW_GROUPS):
            g = jnp.dot(h[r], wg_ref[:, cols], preferred_element_type=F32)
            u = jnp.dot(h[r], wu_ref[:, cols], preferred_element_type=F32)
            a = (g * jax.nn.sigmoid(g) * u).astype(BF16)
            d = jnp.dot(a, wd_ref[cols, :], preferred_element_type=F32)
            acc[r] = d if acc[r] is None else acc[r] + d
    for r in range(FFN_ROW_GROUPS):
        o_ref[rows[r], :] = x[r] + 0.5 * _rms(acc[r], gpost_ref[...])


def _ffn(x2d, g_pre, g_post, wg, wu, wd, to_bf16=()):
    m = x2d.shape[0]
    steps = m // FFN_TM
    tile = pl.BlockSpec((FFN_TM, D_MODEL), lambda i: (i, 0))
    gain = pl.BlockSpec((1, D_MODEL), lambda i: (0, 0))
    slabs = [pl.BlockSpec((w.shape[0] // steps, w.shape[1]), lambda i: (i, 0))
             for w in to_bf16]
    assert all(w.shape[0] % (steps * BF16_SUBLANES) == 0 for w in to_bf16)
    outs = pl.pallas_call(
        functools.partial(_ffn_kernel, len(to_bf16)),
        grid=(steps,),
        in_specs=[tile, gain, gain,
                  _resident((D_MODEL, D_FF)), _resident((D_MODEL, D_FF)),
                  _resident((D_FF, D_MODEL))] + slabs,
        out_specs=[tile] + slabs,
        out_shape=[jax.ShapeDtypeStruct((m, D_MODEL), F32)]
                  + [jax.ShapeDtypeStruct(w.shape, BF16) for w in to_bf16],
        compiler_params=pltpu.CompilerParams(
            dimension_semantics=("arbitrary",), vmem_limit_bytes=VMEM_LIMIT),
        name="ffn",
    )(x2d, g_pre, g_post, wg, wu, wd, *to_bf16)
    return outs[0], outs[1:]


def _mix_in_kernel(x_ref, g_ref, w_ref, cos_ref, sin_ref,
                   qsb_ref, ksb_ref, vsb_ref, qr_ref, kr_ref, vr_ref, gr_ref):
    for r in range(MIX_ROW_GROUPS):
        rows = slice(r * MIX_TM // MIX_ROW_GROUPS, (r + 1) * MIX_TM // MIX_ROW_GROUPS)
        h = _rms(x_ref[rows, :], g_ref[...]).astype(BF16)

        def mm(lo, width):
            return jnp.dot(h, w_ref[:, lo:lo + width], preferred_element_type=F32)

        def rotary(t):
            cosf = cos_ref[rows, :]
            sinf = sin_ref[rows, :]
            parts = []
            for hd in range(RET_HEADS):
                th = t[:, hd * LANES:(hd + 1) * LANES]
                parts.append(th * cosf + pltpu.roll(th, LANES // 2, axis=1) * sinf)
            return jnp.concatenate(parts, axis=1)

        c = 0
        qsb_ref[rows, :] = (mm(c, SB_WIDTH) * SB_Q_SCALE).astype(BF16); c += SB_WIDTH
        ksb_ref[rows, :] = mm(c, SB_WIDTH).astype(BF16); c += SB_WIDTH
        vsb_ref[rows, :] = mm(c, SB_WIDTH).astype(BF16); c += SB_WIDTH
        qr_ref[rows, :] = rotary(mm(c, RET_WIDTH)).astype(BF16); c += RET_WIDTH
        kr_ref[rows, :] = (rotary(mm(c, RET_WIDTH))
                           * (RET_KEY_DIM ** -0.5)).astype(BF16); c += RET_WIDTH
        vr_ref[rows, :] = mm(c, RET_WIDTH).astype(BF16); c += RET_WIDTH
        gr_ref[rows, :] = mm(c, RET_WIDTH)


def _mix_in(x2d, g, w_in, cosf, sinf, seq):
    m = x2d.shape[0]
    n_in = w_in.shape[1]
    pos_blocks = seq // MIX_TM

    def tile(width):
        return pl.BlockSpec((MIX_TM, width), lambda i: (i, 0))

    rope = pl.BlockSpec((MIX_TM, LANES), lambda i: (i % pos_blocks, 0))
    widths = [SB_WIDTH] * 3 + [RET_WIDTH] * 4
    dtypes = [BF16] * 6 + [F32]
    return pl.pallas_call(
        _mix_in_kernel,
        grid=(m // MIX_TM,),
        in_specs=[tile(D_MODEL), pl.BlockSpec((1, D_MODEL), lambda i: (0, 0)),
                  _resident((D_MODEL, n_in)), rope, rope],
        out_specs=[tile(w) for w in widths],
        out_shape=[jax.ShapeDtypeStruct((m, w), dt) for w, dt in zip(widths, dtypes)],
        compiler_params=pltpu.CompilerParams(
            dimension_semantics=("arbitrary",), vmem_limit_bytes=VMEM_LIMIT),
        name="mix_in",
    )(x2d, g, w_in, cosf, sinf)


def _sb_kernel(dq_ref, dk_ref, eq_ref, ek_ref, q_ref, k_ref, v_ref, tri_ref, o_ref,
               qh_ref, kh_ref, vh_ref, t_ref, pre_ref, tot_ref, w_ref, later_ref, acc_ref):
    seq = q_ref.shape[1]
    pairs = q_ref.shape[2] // LANES
    seq_tiles = seq // SB_TQ
    n_tiles = pairs * seq_tiles
    lane = lax.broadcasted_iota(jnp.int32, (SB_TQ, LANES), 1)
    head0 = lane < SB_HEAD_DIM
    row = lax.broadcasted_iota(jnp.int32, (SB_TQ, SB_TK), 0)
    col = lax.broadcasted_iota(jnp.int32, (SB_TQ, SB_TK), 1)
    rel = [col - row + c * SB_TK for c in range(SB_NB)]
    tri = tri_ref[...]

    def tile_rows(tile):
        return pl.ds(pl.multiple_of(tile * SB_TQ, SB_TQ), SB_TQ)

    for tile in range(n_tiles):
        src = (0, slice((tile % seq_tiles) * SB_TQ, (tile % seq_tiles + 1) * SB_TQ),
               slice((tile // seq_tiles) * LANES, (tile // seq_tiles + 1) * LANES))
        dst = slice(tile * SB_TQ, (tile + 1) * SB_TQ)
        q = q_ref[src]
        v = v_ref[src]
        zero = jnp.zeros_like(q)
        qh_ref[0, dst, :] = jnp.where(head0, q, zero)
        qh_ref[1, dst, :] = jnp.where(head0, zero, q)
        kh_ref[dst, :] = k_ref[src]
        vh_ref[tile, :SB_SPAN, :] = jnp.where(head0, v, zero)
        vh_ref[tile, SB_SPAN:, :] = jnp.where(head0, zero, v)
    w_ref[1] = jnp.zeros(w_ref.shape[1:], w_ref.dtype)

    def scores(hd, qt, kt):
        return lax.dot_general(qh_ref[hd, tile_rows(qt), :], kh_ref[tile_rows(kt), :],
                               (((1,), (1,)), ((), ())), preferred_element_type=F32)

    def suffix_sums(t, diagonal):
        sp = jnp.maximum(t, 0.0) + jnp.log2(1.0 + jnp.exp2(-jnp.abs(t)))
        if diagonal:
            sp = jnp.concatenate(
                [jnp.where(rel[c] < 0, sp[:, c * SB_TK:(c + 1) * SB_TK], 0.0)
                 for c in range(SB_NB)], axis=1)
        return jnp.dot(sp.astype(BF16), tri, preferred_element_type=F32)

    def stage_sums(t, sums, hd, slot):
        pre_ref[slot, hd] = t + sums
        tot_ref[slot, hd] = jnp.broadcast_to(sums[:, 0:1], (SB_TQ, SB_TK))

    def weights(hd, slot, qt, diagonal):
        run = None if diagonal else later_ref[qt, hd]
        for c in range(SB_NB):
            arg = pre_ref[slot, hd, :, c * SB_TK:(c + 1) * SB_TK]
            w = jnp.exp2(arg if run is None else arg + run)
            if diagonal:
                w = jnp.where(rel[c] < 0, w, 0.0)
            lo = hd * SB_SPAN + c * SB_TK
            w_ref[slot, :, lo:lo + SB_TK] = w.astype(BF16)
        tot = tot_ref[slot, hd]
        later_ref[qt, hd] = tot if run is None else run + tot

    def sweep(sq_ref, sk_ref, n_steps, diagonal):
        def step(s):
            return sq_ref[s + 1], sk_ref[s + 1]

        def weighted_values(s, slot):
            qt, kt = step(s)
            pv = jnp.dot(w_ref[slot], vh_ref[kt], preferred_element_type=F32)
            return qt, pv

        def accumulate(qt, pv):
            if diagonal:
                acc_ref[qt] = pv
            else:
                acc_ref[qt] += pv

        def half(s, slot):
            t = [t_ref[1 - slot, hd] for hd in range(2)]
            t_new = [scores(0, *step(s + 2))]
            sums = [suffix_sums(t[0], diagonal)]
            qt_prev, pv = weighted_values(s - 1, 1 - slot)
            t_new.append(scores(1, *step(s + 2)))
            sums.append(suffix_sums(t[1], diagonal))
            accumulate(qt_prev, pv)
            qt, _ = step(s)
            for hd in range(2):
                weights(hd, slot, qt, diagonal)
            for hd in range(2):
                t_ref[slot, hd] = t_new[hd]
            for hd in range(2):
                stage_sums(t[hd], sums[hd], hd, 1 - slot)

        for hd in range(2):
            t0 = scores(hd, *step(0))
            stage_sums(t0, suffix_sums(t0, diagonal), hd, 0)
        for hd in range(2):
            t_ref[1, hd] = scores(hd, *step(1))

        unroll = _sb_unroll(n_steps)

        def body(i, carry):
            for u in range(unroll):
                half(unroll * i + u, u % 2)
            return carry

        lax.fori_loop(0, n_steps // unroll, body, 0)
        accumulate(*weighted_values(n_steps - 1, 1))

    sweep(dq_ref, dk_ref, n_tiles, True)
    sweep(eq_ref, ek_ref, pairs * seq_tiles * (seq_tiles - 1) // 2, False)
    for tile in range(n_tiles):
        o_ref[0, (tile % seq_tiles) * SB_TQ:(tile % seq_tiles + 1) * SB_TQ,
              (tile // seq_tiles) * LANES:(tile // seq_tiles + 1) * LANES] = (
                  acc_ref[tile].astype(o_ref.dtype))


def _sb_unroll(n_steps):
    return max(u for u in range(2, SB_MAX_UNROLL + 1, 2) if n_steps % u == 0)


def _sb_steps(pairs, seq_tiles):
    n_tiles = pairs * seq_tiles
    diag = [(g, g) for g in range(n_tiles)]
    early = [(p * seq_tiles + t, p * seq_tiles + kt) for p in range(pairs)
             for t in range(1, seq_tiles) for kt in range(t - 1, -1, -1)]
    tables = []
    for steps in (diag, early):
        steps = [(n_tiles, 0)] + steps + [steps[-1]] * 2
        tables += [jnp.asarray([s[0] for s in steps], jnp.int32),
                   jnp.asarray([s[1] for s in steps], jnp.int32)]
    return tables


def _sb_attention(q, k, v, tri):
    b, seq, width = q.shape
    pairs = width // LANES
    seq_tiles = seq // SB_TQ
    n_tiles = pairs * seq_tiles
    spec = pl.BlockSpec((1, seq, width), lambda bi: (bi, 0, 0))
    smem = pl.BlockSpec(memory_space=pltpu.SMEM)
    return pl.pallas_call(
        _sb_kernel,
        grid=(b,),
        in_specs=[smem, smem, smem, smem, spec, spec, spec,
                  pl.BlockSpec((SB_SPAN, SB_SPAN), lambda bi: (0, 0))],
        out_specs=spec,
        out_shape=jax.ShapeDtypeStruct((b, seq, width), BF16),
        scratch_shapes=[pltpu.VMEM((2, n_tiles * SB_TQ, LANES), BF16),
                        pltpu.VMEM((n_tiles * SB_TQ, LANES), BF16),
                        pltpu.VMEM((n_tiles, 2 * SB_SPAN, LANES), BF16),
                        pltpu.VMEM((2, 2, SB_TQ, SB_SPAN), F32),
                        pltpu.VMEM((2, 2, SB_TQ, SB_SPAN), F32),
                        pltpu.VMEM((2, 2, SB_TQ, SB_TK), F32),
                        pltpu.VMEM((2, SB_TQ, 2 * SB_SPAN), BF16),
                        pltpu.VMEM((n_tiles, 2, SB_TQ, SB_TK), F32),
                        pltpu.VMEM((n_tiles + 1, SB_TQ, LANES), F32)],
        compiler_params=pltpu.CompilerParams(
            dimension_semantics=("arbitrary",), vmem_limit_bytes=VMEM_LIMIT),
        name="sb_attn",
    )(*_sb_steps(pairs, seq_tiles), q, k, v, tri)


def _ret_kernel(lg_ref, q_ref, k_ref, v_ref, g_ref, o_ref,
                decay_ref, qdec_ref, kdec_ref, state_ref):
    seq = q_ref.shape[1]
    c = RET_CHUNK
    heads = range(RET_HEADS)
    diff = (lax.broadcasted_iota(jnp.int32, (c, c), 0)
            - lax.broadcasted_iota(jnp.int32, (c, c), 1)).astype(F32)
    idx = lax.broadcasted_iota(jnp.int32, (c, LANES), 0).astype(F32)
    for h in heads:
        lg = lg_ref[h]
        decay_ref[h] = jnp.where(diff >= 0, jnp.exp(jnp.maximum(diff, 0.0) * lg), 0.0)
        qdec_ref[h] = jnp.exp((idx + 1.0) * lg)
        kdec_ref[h] = jnp.exp((c - 1.0 - idx) * lg)
    state_ref[...] = jnp.zeros(state_ref.shape, F32)

    def body(n, carry):
        sl = slice(n * c, (n + 1) * c)
        q, k, v = ([ref[0, sl, h * LANES:(h + 1) * LANES] for h in heads]
                   for ref in (q_ref, k_ref, v_ref))
        state = [state_ref[h] for h in heads]
        scores = [lax.dot_general(q[h], k[h], (((1,), (1,)), ((), ())),
                                  preferred_element_type=F32) for h in heads]
        cross = [jnp.dot((q[h].astype(F32) * qdec_ref[h]).astype(BF16),
                         state[h].astype(BF16), preferred_element_type=F32)
                 for h in heads]
        kd_t = [(k[h].astype(F32) * kdec_ref[h]).T.astype(BF16) for h in heads]
        kv = [jnp.dot(kd_t[h], v[h], preferred_element_type=F32) for h in heads]
        inner = [jnp.dot((scores[h] * decay_ref[h]).astype(BF16), v[h],
                         preferred_element_type=F32) for h in heads]
        for h in heads:
            chunk_decay = jnp.exp(jnp.full((1, LANES), float(c), F32) * lg_ref[h])
            state_ref[h] = chunk_decay * state[h] + kv[h]
        for h in heads:
            o = inner[h] + cross[h]
            mu = jnp.mean(o, axis=-1, keepdims=True)
            ctr = o - mu
            var = jnp.mean(ctr * ctr, axis=-1, keepdims=True)
            gate = g_ref[0, sl, h * LANES:(h + 1) * LANES]
            o_ref[0, sl, h * LANES:(h + 1) * LANES] = (
                ctr * lax.rsqrt(var + GN_EPS)
                * (gate * jax.nn.sigmoid(gate))).astype(o_ref.dtype)
        return carry

    for n in range(seq // c):
        body(n, 0)


def _retention(log_gamma, q, k, v, g):
    b, seq, width = q.shape
    spec = pl.BlockSpec((1, seq, width), lambda bi: (bi, 0, 0))
    c = RET_CHUNK
    return pl.pallas_call(
        _ret_kernel,
        grid=(b,),
        in_specs=[pl.BlockSpec(memory_space=pltpu.SMEM), spec, spec, spec, spec],
        out_specs=spec,
        out_shape=jax.ShapeDtypeStruct((b, seq, RET_WIDTH), BF16),
        scratch_shapes=[pltpu.VMEM((RET_HEADS, c, c), F32),
                        pltpu.VMEM((RET_HEADS, c, LANES), F32),
                        pltpu.VMEM((RET_HEADS, c, LANES), F32),
                        pltpu.VMEM((RET_HEADS, RET_KEY_DIM, LANES), F32)],
        compiler_params=pltpu.CompilerParams(
            dimension_semantics=("arbitrary",), vmem_limit_bytes=VMEM_LIMIT),
        name="retention",
    )(log_gamma, q, k, v, g)


def _merge_kernel(x_ref, osb_ref, or_ref, gpre_ref, wgate_ref,
                  wsb_ref, wret_ref, wout_ref, gpost_ref, o_ref):
    lo = wgate_ref.shape[1] - 2 * D_MODEL
    for r in range(MIX_ROW_GROUPS):
        rows = slice(r * MIX_TM // MIX_ROW_GROUPS, (r + 1) * MIX_TM // MIX_ROW_GROUPS)
        x = x_ref[rows, :]
        p_sb = jnp.dot(osb_ref[rows, :], wsb_ref[...], preferred_element_type=F32)
        p_ret = jnp.dot(or_ref[rows, :], wret_ref[...], preferred_element_type=F32)
        h = _rms(x, gpre_ref[...]).astype(BF16)
        gate_sb = jnp.dot(h, wgate_ref[:, lo:lo + D_MODEL], preferred_element_type=F32)
        merged = jax.nn.sigmoid(gate_sb) * p_sb
        gate_ret = jnp.dot(h, wgate_ref[:, lo + D_MODEL:], preferred_element_type=F32)
        merged = merged + jax.nn.sigmoid(gate_ret) * p_ret
        y = jnp.dot(merged.astype(BF16), wout_ref[...], preferred_element_type=F32)
        o_ref[rows, :] = x + _rms(y, gpost_ref[...])


def _merge(x2d, o_sb, o_r, g_pre, w_gates, w_sb, w_ret, w_out, g_post):
    m = x2d.shape[0]

    def tile(width):
        return pl.BlockSpec((MIX_TM, width), lambda i: (i, 0))

    gain = pl.BlockSpec((1, D_MODEL), lambda i: (0, 0))
    return pl.pallas_call(
        _merge_kernel,
        grid=(m // MIX_TM,),
        in_specs=[tile(D_MODEL), tile(SB_WIDTH), tile(RET_WIDTH),
                  gain, _resident(w_gates.shape),
                  _resident((SB_WIDTH, D_MODEL)), _resident((RET_WIDTH, D_MODEL)),
                  _resident((D_MODEL, D_MODEL)), gain],
        out_specs=tile(D_MODEL),
        out_shape=jax.ShapeDtypeStruct((m, D_MODEL), F32),
        compiler_params=pltpu.CompilerParams(
            dimension_semantics=("arbitrary",), vmem_limit_bytes=VMEM_LIMIT),
        name="merge",
    )(x2d, o_sb, o_r, g_pre, w_gates, w_sb, w_ret, w_out, g_post)


def _rope_tables(seq):
    half = RET_KEY_DIM // 2
    pos = jnp.arange(seq, dtype=F32)
    inv_freq = ROPE_BASE ** (-jnp.arange(half, dtype=F32) / half)
    ang = pos[:, None] * inv_freq[None, :]
    cos, sin = jnp.cos(ang), jnp.sin(ang)
    return (jnp.concatenate([cos, cos], axis=-1),
            jnp.concatenate([-sin, sin], axis=-1))


def _suffix_sum_matrix():
    j = np.arange(SB_SPAN)[:, None]
    s = np.arange(SB_SPAN)[None, :]
    return jnp.asarray(-(j >= s).astype(np.float32), dtype=BF16)


def kernel(x, g_ffn1_pre, g_ffn1_post, w_ffn1_gate, w_ffn1_up, w_ffn1_down,
           g_mix_pre, w_in, w_proj_sb, w_proj_ret, w_out, g_mix_post,
           g_ffn2_pre, g_ffn2_post, w_ffn2_gate, w_ffn2_up, w_ffn2_down):
    b, seq, d = x.shape
    depth = w_in.shape[0]
    log_gamma = jnp.log(1.0 - 2.0 ** (-5.0 - jnp.arange(RET_HEADS, dtype=F32)))
    cosf, sinf = _rope_tables(seq)
    tri = _suffix_sum_matrix()
    x2d = x.reshape(b * seq, d)
    for l in range(depth):
        x2d, (w_in_l, w_sb_l, w_ret_l, w_out_l, w2_gate, w2_up, w2_down) = _ffn(
            x2d, g_ffn1_pre[l][None], g_ffn1_post[l][None],
            w_ffn1_gate[l].astype(BF16), w_ffn1_up[l].astype(BF16),
            w_ffn1_down[l].astype(BF16),
            to_bf16=(w_in[l], w_proj_sb[l], w_proj_ret[l], w_out[l],
                     w_ffn2_gate[l], w_ffn2_up[l], w_ffn2_down[l]))
        q_sb, k_sb, v_sb, q_r, k_r, v_r, g_r = _mix_in(
            x2d, g_mix_pre[l][None], w_in_l, cosf, sinf, seq)
        shp = (b, seq, SB_WIDTH)
        o_sb = _sb_attention(q_sb.reshape(shp), k_sb.reshape(shp),
                             v_sb.reshape(shp), tri)
        o_r = _retention(log_gamma, q_r.reshape(shp), k_r.reshape(shp),
                         v_r.reshape(shp), g_r.reshape(shp))
        x2d = _merge(x2d, o_sb.reshape(b * seq, SB_WIDTH),
                     o_r.reshape(b * seq, RET_WIDTH),
                     g_mix_pre[l][None], w_in_l, w_sb_l, w_ret_l, w_out_l,
                     g_mix_post[l][None])
        x2d, _ = _ffn(x2d, g_ffn2_pre[l][None], g_ffn2_post[l][None],
                      w2_gate, w2_up, w2_down)
    return x2d.reshape(b, seq, d)
```

```python
import functools
import math

import numpy as np
import jax
import jax.numpy as jnp
from jax import lax
from jax.experimental import pallas as pl
from jax.experimental.pallas import tpu as pltpu

F32 = jnp.float32
BF16 = jnp.bfloat16

D_MODEL = 1024
D_FF = 4 * D_MODEL
SB_HEADS = 8
SB_HEAD_DIM = 64
SB_WIDTH = SB_HEADS * SB_HEAD_DIM
RET_HEADS = 4
RET_KEY_DIM = 128
RET_WIDTH = RET_HEADS * RET_KEY_DIM
RMS_EPS = 1e-6
GN_EPS = 1e-5
ROPE_BASE = 10000.0

LANES = 128
BF16_SUBLANES = 16
VMEM_LIMIT = 56 * 1024 * 1024

FFN_TM = 1024
FFN_TM_WITH_CASTS = 512
FFN_TF = 1024
MIX_TM = 1024
ROW_GROUP = 256
SB_TQ = 256
SB_TK = LANES
SB_SPAN = SB_TQ
SB_NB = SB_SPAN // SB_TK
SB_MAX_UNROLL = 14
SB_Q_SCALE = SB_HEAD_DIM ** -0.5 * math.log2(math.e)
RET_CHUNK = 256


def _rms(x, g):
    ms = jnp.mean(x * x, axis=-1, keepdims=True)
    return x * lax.rsqrt(ms + RMS_EPS) * g


def _resident(shape):
    nd = len(shape)
    return pl.BlockSpec(shape, lambda *_: (0,) * nd, pipeline_mode=pl.Buffered(1))


def _ffn_kernel(n_casts, x_ref, gpre_ref, gpost_ref, wg_ref, wu_ref, wd_ref, *refs):
    cast_in, o_ref, cast_out = refs[:n_casts], refs[n_casts], refs[n_casts + 1:]
    for src, dst in zip(cast_in, cast_out):
        dst[...] = src[...].astype(dst.dtype)
    groups = x_ref.shape[0] // ROW_GROUP
    rows = [slice(r * ROW_GROUP, (r + 1) * ROW_GROUP) for r in range(groups)]
    x = [x_ref[r, :] for r in rows]
    h = [_rms(xr, gpre_ref[...]).astype(BF16) for xr in x]
    acc = [None] * groups
    for j in range(D_FF // FFN_TF):
        cols = slice(j * FFN_TF, (j + 1) * FFN_TF)
        for r in range(groups):
            g = jnp.dot(h[r], wg_ref[:, cols], preferred_element_type=F32)
            u = jnp.dot(h[r], wu_ref[:, cols], preferred_element_type=F32)
            a = (g * jax.nn.sigmoid(g) * u).astype(BF16)
            d = jnp.dot(a, wd_ref[cols, :], preferred_element_type=F32)
            acc[r] = d if acc[r] is None else acc[r] + d
    for r in range(groups):
        o_ref[rows[r], :] = x[r] + 0.5 * _rms(acc[r], gpost_ref[...])


def _ffn(x2d, g_pre, g_post, wg, wu, wd, to_bf16=()):
    m = x2d.shape[0]
    tm = FFN_TM_WITH_CASTS if to_bf16 else FFN_TM
    steps = m // tm
    tile = pl.BlockSpec((tm, D_MODEL), lambda i: (i, 0))
    gain = pl.BlockSpec((1, D_MODEL), lambda i: (0, 0))
    slabs = [pl.BlockSpec((w.shape[0] // steps, w.shape[1]), lambda i: (i, 0))
             for w in to_bf16]
    assert all(w.shape[0] % (steps * BF16_SUBLANES) == 0 for w in to_bf16)
    outs = pl.pallas_call(
        functools.partial(_ffn_kernel, len(to_bf16)),
        grid=(steps,),
        in_specs=[tile, gain, gain,
                  _resident((D_MODEL, D_FF)), _resident((D_MODEL, D_FF)),
                  _resident((D_FF, D_MODEL))] + slabs,
        out_specs=[tile] + slabs,
        out_shape=[jax.ShapeDtypeStruct((m, D_MODEL), F32)]
                  + [jax.ShapeDtypeStruct(w.shape, BF16) for w in to_bf16],
        compiler_params=pltpu.CompilerParams(
            dimension_semantics=("arbitrary",), vmem_limit_bytes=VMEM_LIMIT),
        name="ffn",
    )(x2d, g_pre, g_post, wg, wu, wd, *to_bf16)
    return outs[0], outs[1:]


def _mix_in_kernel(x_ref, g_ref, w_ref, cos_ref, sin_ref,
                   qsb_ref, ksb_ref, vsb_ref, qr_ref, kr_ref, vr_ref, gr_ref):
    for r in range(MIX_TM // ROW_GROUP):
        rows = slice(r * ROW_GROUP, (r + 1) * ROW_GROUP)
        h = _rms(x_ref[rows, :], g_ref[...]).astype(BF16)

        def mm(lo, width):
            return jnp.dot(h, w_ref[:, lo:lo + width], preferred_element_type=F32)

        def rotary(t):
            cosf = cos_ref[rows, :]
            sinf = sin_ref[rows, :]
            parts = []
            for hd in range(RET_HEADS):
                th = t[:, hd * LANES:(hd + 1) * LANES]
                parts.append(th * cosf + pltpu.roll(th, LANES // 2, axis=1) * sinf)
            return jnp.concatenate(parts, axis=1)

        c = 0
        qsb_ref[rows, :] = (mm(c, SB_WIDTH) * SB_Q_SCALE).astype(BF16); c += SB_WIDTH
        ksb_ref[rows, :] = mm(c, SB_WIDTH).astype(BF16); c += SB_WIDTH
        vsb_ref[rows, :] = mm(c, SB_WIDTH).astype(BF16); c += SB_WIDTH
        qr_ref[rows, :] = rotary(mm(c, RET_WIDTH)).astype(BF16); c += RET_WIDTH
        kr_ref[rows, :] = (rotary(mm(c, RET_WIDTH))
                           * (RET_KEY_DIM ** -0.5)).astype(BF16); c += RET_WIDTH
        vr_ref[rows, :] = mm(c, RET_WIDTH).astype(BF16); c += RET_WIDTH
        gr_ref[rows, :] = mm(c, RET_WIDTH)


def _mix_in(x2d, g, w_in, cosf, sinf, seq):
    m = x2d.shape[0]
    n_in = w_in.shape[1]
    pos_blocks = seq // MIX_TM

    def tile(width):
        return pl.BlockSpec((MIX_TM, width), lambda i: (i, 0))

    rope = pl.BlockSpec((MIX_TM, LANES), lambda i: (i % pos_blocks, 0))
    widths = [SB_WIDTH] * 3 + [RET_WIDTH] * 4
    dtypes = [BF16] * 6 + [F32]
    return pl.pallas_call(
        _mix_in_kernel,
        grid=(m // MIX_TM,),
        in_specs=[tile(D_MODEL), pl.BlockSpec((1, D_MODEL), lambda i: (0, 0)),
                  _resident((D_MODEL, n_in)), rope, rope],
        out_specs=[tile(w) for w in widths],
        out_shape=[jax.ShapeDtypeStruct((m, w), dt) for w, dt in zip(widths, dtypes)],
        compiler_params=pltpu.CompilerParams(
            dimension_semantics=("arbitrary",), vmem_limit_bytes=VMEM_LIMIT),
        name="mix_in",
    )(x2d, g, w_in, cosf, sinf)


def _sb_kernel(dq_ref, dk_ref, eq_ref, ek_ref, q_ref, k_ref, v_ref, tri_ref, o_ref,
               qh_ref, kh_ref, vh_ref, t_ref, pre_ref, tot_ref, w_ref, later_ref, acc_ref):
    seq = q_ref.shape[1]
    pairs = q_ref.shape[2] // LANES
    seq_tiles = seq // SB_TQ
    n_tiles = pairs * seq_tiles
    lane = lax.broadcasted_iota(jnp.int32, (SB_TQ, LANES), 1)
    head0 = lane < SB_HEAD_DIM
    row = lax.broadcasted_iota(jnp.int32, (SB_TQ, SB_TK), 0)
    col = lax.broadcasted_iota(jnp.int32, (SB_TQ, SB_TK), 1)
    rel = [col - row + c * SB_TK for c in range(SB_NB)]
    tri = tri_ref[...]

    def tile_rows(tile):
        return pl.ds(pl.multiple_of(tile * SB_TQ, SB_TQ), SB_TQ)

    for tile in range(n_tiles):
        src = (0, slice((tile % seq_tiles) * SB_TQ, (tile % seq_tiles + 1) * SB_TQ),
               slice((tile // seq_tiles) * LANES, (tile // seq_tiles + 1) * LANES))
        dst = slice(tile * SB_TQ, (tile + 1) * SB_TQ)
        q = q_ref[src]
        v = v_ref[src]
        zero = jnp.zeros_like(q)
        qh_ref[0, dst, :] = jnp.where(head0, q, zero)
        qh_ref[1, dst, :] = jnp.where(head0, zero, q)
        kh_ref[dst, :] = k_ref[src]
        vh_ref[tile, :SB_SPAN, :] = jnp.where(head0, v, zero)
        vh_ref[tile, SB_SPAN:, :] = jnp.where(head0, zero, v)
    w_ref[1] = jnp.zeros(w_ref.shape[1:], w_ref.dtype)

    def scores(hd, qt, kt):
        return lax.dot_general(qh_ref[hd, tile_rows(qt), :], kh_ref[tile_rows(kt), :],
                               (((1,), (1,)), ((), ())), preferred_element_type=F32)

    def suffix_sums(t, diagonal):
        sp = jnp.maximum(t, 0.0) + jnp.log2(1.0 + jnp.exp2(-jnp.abs(t)))
        if diagonal:
            sp = jnp.concatenate(
                [jnp.where(rel[c] < 0, sp[:, c * SB_TK:(c + 1) * SB_TK], 0.0)
                 for c in range(SB_NB)], axis=1)
        return jnp.dot(sp.astype(BF16), tri, preferred_element_type=F32)

    def stage_sums(t, sums, hd, slot):
        pre_ref[slot, hd] = t + sums
        tot_ref[slot, hd] = jnp.broadcast_to(sums[:, 0:1], (SB_TQ, SB_TK))

    def weights(hd, slot, qt, diagonal):
        run = None if diagonal else later_ref[qt, hd]
        for c in range(SB_NB):
            arg = pre_ref[slot, hd, :, c * SB_TK:(c + 1) * SB_TK]
            w = jnp.exp2(arg if run is None else arg + run)
            if diagonal:
                w = jnp.where(rel[c] < 0, w, 0.0)
            lo = hd * SB_SPAN + c * SB_TK
            w_ref[slot, :, lo:lo + SB_TK] = w.astype(BF16)
        tot = tot_ref[slot, hd]
        later_ref[qt, hd] = tot if run is None else run + tot

    def sweep(sq_ref, sk_ref, n_steps, diagonal):
        def step(s):
            return sq_ref[s + 1], sk_ref[s + 1]

        def weighted_values(s, slot):
            qt, kt = step(s)
            pv = jnp.dot(w_ref[slot], vh_ref[kt], preferred_element_type=F32)
            return qt, pv

        def accumulate(qt, pv):
            if diagonal:
                acc_ref[qt] = pv
            else:
                acc_ref[qt] += pv

        def half(s, slot):
            t = [t_ref[1 - slot, hd] for hd in range(2)]
            t_new = [scores(0, *step(s + 2))]
            sums = [suffix_sums(t[0], diagonal)]
            qt_prev, pv = weighted_values(s - 1, 1 - slot)
            t_new.append(scores(1, *step(s + 2)))
            sums.append(suffix_sums(t[1], diagonal))
            accumulate(qt_prev, pv)
            qt, _ = step(s)
            for hd in range(2):
                weights(hd, slot, qt, diagonal)
            for hd in range(2):
                t_ref[slot, hd] = t_new[hd]
            for hd in range(2):
                stage_sums(t[hd], sums[hd], hd, 1 - slot)

        for hd in range(2):
            t0 = scores(hd, *step(0))
            stage_sums(t0, suffix_sums(t0, diagonal), hd, 0)
        for hd in range(2):
            t_ref[1, hd] = scores(hd, *step(1))

        unroll = _sb_unroll(n_steps)

        def body(i, carry):
            for u in range(unroll):
                half(unroll * i + u, u % 2)
            return carry

        lax.fori_loop(0, n_steps // unroll, body, 0)
        accumulate(*weighted_values(n_steps - 1, 1))

    sweep(dq_ref, dk_ref, n_tiles, True)
    sweep(eq_ref, ek_ref, pairs * seq_tiles * (seq_tiles - 1) // 2, False)
    for tile in range(n_tiles):
        o_ref[0, (tile % seq_tiles) * SB_TQ:(tile % seq_tiles + 1) * SB_TQ,
              (tile // seq_tiles) * LANES:(tile // seq_tiles + 1) * LANES] = (
                  acc_ref[tile].astype(o_ref.dtype))


def _sb_unroll(n_steps):
    return max(u for u in range(2, SB_MAX_UNROLL + 1, 2) if n_steps % u == 0)


def _sb_steps(pairs, seq_tiles):
    n_tiles = pairs * seq_tiles
    diag = [(g, g) for g in range(n_tiles)]
    early = [(p * seq_tiles + t, p * seq_tiles + kt) for p in range(pairs)
             for t in range(1, seq_tiles) for kt in range(t - 1, -1, -1)]
    tables = []
    for steps in (diag, early):
        steps = [(n_tiles, 0)] + steps + [steps[-1]] * 2
        tables += [jnp.asarray([s[0] for s in steps], jnp.int32),
                   jnp.asarray([s[1] for s in steps], jnp.int32)]
    return tables


def _sb_attention(q, k, v, tri):
    b, seq, width = q.shape
    pairs = width // LANES
    seq_tiles = seq // SB_TQ
    n_tiles = pairs * seq_tiles
    spec = pl.BlockSpec((1, seq, width), lambda bi: (bi, 0, 0))
    smem = pl.BlockSpec(memory_space=pltpu.SMEM)
    return pl.pallas_call(
        _sb_kernel,
        grid=(b,),
        in_specs=[smem, smem, smem, smem, spec, spec, spec,
                  pl.BlockSpec((SB_SPAN, SB_SPAN), lambda bi: (0, 0))],
        out_specs=spec,
        out_shape=jax.ShapeDtypeStruct((b, seq, width), BF16),
        scratch_shapes=[pltpu.VMEM((2, n_tiles * SB_TQ, LANES), BF16),
                        pltpu.VMEM((n_tiles * SB_TQ, LANES), BF16),
                        pltpu.VMEM((n_tiles, 2 * SB_SPAN, LANES), BF16),
                        pltpu.VMEM((2, 2, SB_TQ, SB_SPAN), F32),
                        pltpu.VMEM((2, 2, SB_TQ, SB_SPAN), F32),
                        pltpu.VMEM((2, 2, SB_TQ, SB_TK), F32),
                        pltpu.VMEM((2, SB_TQ, 2 * SB_SPAN), BF16),
                        pltpu.VMEM((n_tiles, 2, SB_TQ, SB_TK), F32),
                        pltpu.VMEM((n_tiles + 1, SB_TQ, LANES), F32)],
        compiler_params=pltpu.CompilerParams(
            dimension_semantics=("arbitrary",), vmem_limit_bytes=VMEM_LIMIT),
        name="sb_attn",
    )(*_sb_steps(pairs, seq_tiles), q, k, v, tri)


def _ret_kernel(lg_ref, q_ref, k_ref, v_ref, g_ref, o_ref,
                decay_ref, qdec_ref, kdec_ref, state_ref):
    seq = q_ref.shape[1]
    c = RET_CHUNK
    heads = range(RET_HEADS)
    diff = (lax.broadcasted_iota(jnp.int32, (c, c), 0)
            - lax.broadcasted_iota(jnp.int32, (c, c), 1)).astype(F32)
    idx = lax.broadcasted_iota(jnp.int32, (c, LANES), 0).astype(F32)
    for h in heads:
        lg = lg_ref[h]
        decay_ref[h] = jnp.where(diff >= 0, jnp.exp(jnp.maximum(diff, 0.0) * lg), 0.0)
        qdec_ref[h] = jnp.exp((idx + 1.0) * lg)
        kdec_ref[h] = jnp.exp((c - 1.0 - idx) * lg)
    state_ref[...] = jnp.zeros(state_ref.shape, F32)

    def body(n, carry):
        sl = slice(n * c, (n + 1) * c)
        q, k, v = ([ref[0, sl, h * LANES:(h + 1) * LANES] for h in heads]
                   for ref in (q_ref, k_ref, v_ref))
        state = [state_ref[h] for h in heads]
        scores = [lax.dot_general(q[h], k[h], (((1,), (1,)), ((), ())),
                                  preferred_element_type=F32) for h in heads]
        cross = [jnp.dot((q[h].astype(F32) * qdec_ref[h]).astype(BF16),
                         state[h].astype(BF16), preferred_element_type=F32)
                 for h in heads]
        kd_t = [(k[h].astype(F32) * kdec_ref[h]).T.astype(BF16) for h in heads]
        kv = [jnp.dot(kd_t[h], v[h], preferred_element_type=F32) for h in heads]
        inner = [jnp.dot((scores[h] * decay_ref[h]).astype(BF16), v[h],
                         preferred_element_type=F32) for h in heads]
        for h in heads:
            chunk_decay = jnp.exp(jnp.full((1, LANES), float(c), F32) * lg_ref[h])
            state_ref[h] = chunk_decay * state[h] + kv[h]
        for h in heads:
            o = inner[h] + cross[h]
            mu = jnp.mean(o, axis=-1, keepdims=True)
            ctr = o - mu
            var = jnp.mean(ctr * ctr, axis=-1, keepdims=True)
            gate = g_ref[0, sl, h * LANES:(h + 1) * LANES]
            o_ref[0, sl, h * LANES:(h + 1) * LANES] = (
                ctr * lax.rsqrt(var + GN_EPS)
                * (gate * jax.nn.sigmoid(gate))).astype(o_ref.dtype)
        return carry

    for n in range(seq // c):
        body(n, 0)


def _retention(log_gamma, q, k, v, g):
    b, seq, width = q.shape
    spec = pl.BlockSpec((1, seq, width), lambda bi: (bi, 0, 0))
    c = RET_CHUNK
    return pl.pallas_call(
        _ret_kernel,
        grid=(b,),
        in_specs=[pl.BlockSpec(memory_space=pltpu.SMEM), spec, spec, spec, spec],
        out_specs=spec,
        out_shape=jax.ShapeDtypeStruct((b, seq, RET_WIDTH), BF16),
        scratch_shapes=[pltpu.VMEM((RET_HEADS, c, c), F32),
                        pltpu.VMEM((RET_HEADS, c, LANES), F32),
                        pltpu.VMEM((RET_HEADS, c, LANES), F32),
                        pltpu.VMEM((RET_HEADS, RET_KEY_DIM, LANES), F32)],
        compiler_params=pltpu.CompilerParams(
            dimension_semantics=("arbitrary",), vmem_limit_bytes=VMEM_LIMIT),
        name="retention",
    )(log_gamma, q, k, v, g)


def _merge_kernel(x_ref, osb_ref, or_ref, gpre_ref, wgate_ref,
                  wsb_ref, wret_ref, wout_ref, gpost_ref, o_ref):
    lo = wgate_ref.shape[1] - 2 * D_MODEL
    for r in range(MIX_TM // ROW_GROUP):
        rows = slice(r * ROW_GROUP, (r + 1) * ROW_GROUP)
        x = x_ref[rows, :]
        p_sb = jnp.dot(osb_ref[rows, :], wsb_ref[...], preferred_element_type=F32)
        p_ret = jnp.dot(or_ref[rows, :], wret_ref[...], preferred_element_type=F32)
        h = _rms(x, gpre_ref[...]).astype(BF16)
        gate_sb = jnp.dot(h, wgate_ref[:, lo:lo + D_MODEL], preferred_element_type=F32)
        merged = jax.nn.sigmoid(gate_sb) * p_sb
        gate_ret = jnp.dot(h, wgate_ref[:, lo + D_MODEL:], preferred_element_type=F32)
        merged = merged + jax.nn.sigmoid(gate_ret) * p_ret
        y = jnp.dot(merged.astype(BF16), wout_ref[...], preferred_element_type=F32)
        o_ref[rows, :] = x + _rms(y, gpost_ref[...])


def _merge(x2d, o_sb, o_r, g_pre, w_gates, w_sb, w_ret, w_out, g_post):
    m = x2d.shape[0]

    def tile(width):
        return pl.BlockSpec((MIX_TM, width), lambda i: (i, 0))

    gain = pl.BlockSpec((1, D_MODEL), lambda i: (0, 0))
    return pl.pallas_call(
        _merge_kernel,
        grid=(m // MIX_TM,),
        in_specs=[tile(D_MODEL), tile(SB_WIDTH), tile(RET_WIDTH),
                  gain, _resident(w_gates.shape),
                  _resident((SB_WIDTH, D_MODEL)), _resident((RET_WIDTH, D_MODEL)),
                  _resident((D_MODEL, D_MODEL)), gain],
        out_specs=tile(D_MODEL),
        out_shape=jax.ShapeDtypeStruct((m, D_MODEL), F32),
        compiler_params=pltpu.CompilerParams(
            dimension_semantics=("arbitrary",), vmem_limit_bytes=VMEM_LIMIT),
        name="merge",
    )(x2d, o_sb, o_r, g_pre, w_gates, w_sb, w_ret, w_out, g_post)


def _rope_tables(seq):
    half = RET_KEY_DIM // 2
    pos = np.arange(seq, dtype=np.float64)
    inv_freq = ROPE_BASE ** (-np.arange(half, dtype=np.float64) / half)
    ang = pos[:, None] * inv_freq[None, :]
    cos, sin = np.cos(ang), np.sin(ang)
    return (jnp.asarray(np.concatenate([cos, cos], axis=-1), dtype=F32),
            jnp.asarray(np.concatenate([-sin, sin], axis=-1), dtype=F32))


def _suffix_sum_matrix():
    j = np.arange(SB_SPAN)[:, None]
    s = np.arange(SB_SPAN)[None, :]
    return jnp.asarray(-(j >= s).astype(np.float32), dtype=BF16)


def kernel(x, g_ffn1_pre, g_ffn1_post, w_ffn1_gate, w_ffn1_up, w_ffn1_down,
           g_mix_pre, w_in, w_proj_sb, w_proj_ret, w_out, g_mix_post,
           g_ffn2_pre, g_ffn2_post, w_ffn2_gate, w_ffn2_up, w_ffn2_down):
    b, seq, d = x.shape
    depth = w_in.shape[0]
    log_gamma = jnp.log(1.0 - 2.0 ** (-5.0 - jnp.arange(RET_HEADS, dtype=F32)))
    cosf, sinf = _rope_tables(seq)
    tri = _suffix_sum_matrix()
    x2d = x.reshape(b * seq, d)
    for l in range(depth):
        x2d, (w_in_l, w_sb_l, w_ret_l, w_out_l, w2_gate, w2_up, w2_down) = _ffn(
            x2d, g_ffn1_pre[l][None], g_ffn1_post[l][None],
            w_ffn1_gate[l].astype(BF16), w_ffn1_up[l].astype(BF16),
            w_ffn1_down[l].astype(BF16),
            to_bf16=(w_in[l], w_proj_sb[l], w_proj_ret[l], w_out[l],
                     w_ffn2_gate[l], w_ffn2_up[l], w_ffn2_down[l]))
        q_sb, k_sb, v_sb, q_r, k_r, v_r, g_r = _mix_in(
            x2d, g_mix_pre[l][None], w_in_l, cosf, sinf, seq)
        shp = (b, seq, SB_WIDTH)
        o_sb = _sb_attention(q_sb.reshape(shp), k_sb.reshape(shp),
                             v_sb.reshape(shp), tri)
        o_r = _retention(log_gamma, q_r.reshape(shp), k_r.reshape(shp),
                         v_r.reshape(shp), g_r.reshape(shp))
        x2d = _merge(x2d, o_sb.reshape(b * seq, SB_WIDTH),
                     o_r.reshape(b * seq, RET_WIDTH),
                     g_mix_pre[l][None], w_in_l, w_sb_l, w_ret_l, w_out_l,
                     g_mix_post[l][None])
        x2d, _ = _ffn(x2d, g_ffn2_pre[l][None], g_ffn2_post[l][None],
                      w2_gate, w2_up, w2_down)
    return x2d.reshape(b, seq, d)
```

```python
import functools
import math

import numpy as np
import jax
import jax.numpy as jnp
from jax import lax
from jax.experimental import pallas as pl
from jax.experimental.pallas import tpu as pltpu

F32 = jnp.float32
BF16 = jnp.bfloat16

D_MODEL = 1024
D_FF = 4 * D_MODEL
SB_HEADS = 8
SB_HEAD_DIM = 64
SB_WIDTH = SB_HEADS * SB_HEAD_DIM
RET_HEADS = 4
RET_KEY_DIM = 128
RET_WIDTH = RET_HEADS * RET_KEY_DIM
RMS_EPS = 1e-6
GN_EPS = 1e-5
ROPE_BASE = 10000.0

LANES = 128
BF16_SUBLANES = 16
VMEM_LIMIT = 56 * 1024 * 1024

FFN_TM = 1024
FFN_TF = 1024
MIX_IN_TM = 512
MIX_TM = 1024
ROW_GROUP = 256
SB_TQ = 256
SB_TK = LANES
SB_SPAN = SB_TQ
SB_NB = SB_SPAN // SB_TK
SB_MAX_UNROLL = 14
SB_Q_SCALE = SB_HEAD_DIM ** -0.5 * math.log2(math.e)
RET_CHUNK = 256


def _rms(x, g):
    ms = jnp.mean(x * x, axis=-1, keepdims=True)
    return x * lax.rsqrt(ms + RMS_EPS) * g


def _resident(shape):
    nd = len(shape)
    return pl.BlockSpec(shape, lambda *_: (0,) * nd, pipeline_mode=pl.Buffered(1))


def _cast_slabs(srcs, dsts):
    for src, dst in zip(srcs, dsts):
        dst[...] = src[...].astype(dst.dtype)


def _slab_specs(arrays, steps):
    assert all(w.shape[0] % (steps * BF16_SUBLANES) == 0 for w in arrays)
    return [pl.BlockSpec((w.shape[0] // steps, w.shape[1]), lambda i: (i, 0))
            for w in arrays]


def _ffn_kernel(x_ref, gpre_ref, gpost_ref, wg_ref, wu_ref, wd_ref, o_ref):
    groups = x_ref.shape[0] // ROW_GROUP
    rows = [slice(r * ROW_GROUP, (r + 1) * ROW_GROUP) for r in range(groups)]
    x = [x_ref[r, :] for r in rows]
    h = [_rms(xr, gpre_ref[...]).astype(BF16) for xr in x]
    acc = [None] * groups
    for j in range(D_FF // FFN_TF):
        cols = slice(j * FFN_TF, (j + 1) * FFN_TF)
        for r in range(groups):
            g = jnp.dot(h[r], wg_ref[:, cols], preferred_element_type=F32)
            u = jnp.dot(h[r], wu_ref[:, cols], preferred_element_type=F32)
            a = (g * jax.nn.sigmoid(g) * u).astype(BF16)
            d = jnp.dot(a, wd_ref[cols, :], preferred_element_type=F32)
            acc[r] = d if acc[r] is None else acc[r] + d
    for r in range(groups):
        o_ref[rows[r], :] = x[r] + 0.5 * _rms(acc[r], gpost_ref[...])


def _ffn(x2d, g_pre, g_post, wg, wu, wd):
    m = x2d.shape[0]
    tile = pl.BlockSpec((FFN_TM, D_MODEL), lambda i: (i, 0))
    gain = pl.BlockSpec((1, D_MODEL), lambda i: (0, 0))
    return pl.pallas_call(
        _ffn_kernel,
        grid=(m // FFN_TM,),
        in_specs=[tile, gain, gain,
                  _resident((D_MODEL, D_FF)), _resident((D_MODEL, D_FF)),
                  _resident((D_FF, D_MODEL))],
        out_specs=tile,
        out_shape=jax.ShapeDtypeStruct((m, D_MODEL), F32),
        compiler_params=pltpu.CompilerParams(
            dimension_semantics=("arbitrary",), vmem_limit_bytes=VMEM_LIMIT),
        name="ffn",
    )(x2d, g_pre, g_post, wg, wu, wd)

def _mix_in_kernel(n_casts, x_ref, g_ref, w32_ref, cos_ref, sin_ref, *refs):
    cast_in, refs = refs[:n_casts], refs[n_casts:]
    qsb_ref, ksb_ref, vsb_ref, qr_ref, kr_ref, vr_ref, gr_ref = refs[:7]
    cast_out, w_ref = refs[7:7 + n_casts], refs[7 + n_casts]

    @pl.when(pl.program_id(0) == 0)
    def _():
        w_ref[...] = w32_ref[...].astype(BF16)

    _cast_slabs(cast_in, cast_out)
    for r in range(MIX_IN_TM // ROW_GROUP):
        rows = slice(r * ROW_GROUP, (r + 1) * ROW_GROUP)
        h = _rms(x_ref[rows, :], g_ref[...]).astype(BF16)

        def mm(lo, width):
            return jnp.dot(h, w_ref[:, lo:lo + width], preferred_element_type=F32)

        def rotary(t):
            cosf = cos_ref[rows, :]
            sinf = sin_ref[rows, :]
            parts = []
            for hd in range(RET_HEADS):
                th = t[:, hd * LANES:(hd + 1) * LANES]
                parts.append(th * cosf + pltpu.roll(th, LANES // 2, axis=1) * sinf)
            return jnp.concatenate(parts, axis=1)

        c = 0
        qsb_ref[rows, :] = (mm(c, SB_WIDTH) * SB_Q_SCALE).astype(BF16); c += SB_WIDTH
        ksb_ref[rows, :] = mm(c, SB_WIDTH).astype(BF16); c += SB_WIDTH
        vsb_ref[rows, :] = mm(c, SB_WIDTH).astype(BF16); c += SB_WIDTH
        qr_ref[rows, :] = rotary(mm(c, RET_WIDTH)).astype(BF16); c += RET_WIDTH
        kr_ref[rows, :] = (rotary(mm(c, RET_WIDTH))
                           * (RET_KEY_DIM ** -0.5)).astype(BF16); c += RET_WIDTH
        vr_ref[rows, :] = mm(c, RET_WIDTH).astype(BF16); c += RET_WIDTH
        gr_ref[rows, :] = mm(c, RET_WIDTH)


def _mix_in(x2d, g, w_in, cosf, sinf, seq, to_bf16=()):
    m = x2d.shape[0]
    steps = m // MIX_IN_TM
    pos_blocks = seq // MIX_IN_TM

    def tile(width):
        return pl.BlockSpec((MIX_IN_TM, width), lambda i: (i, 0))

    rope = pl.BlockSpec((MIX_IN_TM, LANES), lambda i: (i % pos_blocks, 0))
    widths = [SB_WIDTH] * 3 + [RET_WIDTH] * 4
    dtypes = [BF16] * 6 + [F32]
    n_cols = sum(widths)
    slabs = _slab_specs(to_bf16, steps)
    outs = pl.pallas_call(
        functools.partial(_mix_in_kernel, len(to_bf16)),
        grid=(steps,),
        in_specs=[tile(D_MODEL), pl.BlockSpec((1, D_MODEL), lambda i: (0, 0)),
                  _resident((D_MODEL, n_cols)), rope, rope] + slabs,
        out_specs=[tile(w) for w in widths] + slabs,
        out_shape=[jax.ShapeDtypeStruct((m, w), dt) for w, dt in zip(widths, dtypes)]
                  + [jax.ShapeDtypeStruct(w.shape, BF16) for w in to_bf16],
        scratch_shapes=[pltpu.VMEM((D_MODEL, n_cols), BF16)],
        compiler_params=pltpu.CompilerParams(
            dimension_semantics=("arbitrary",), vmem_limit_bytes=VMEM_LIMIT),
        name="mix_in",
    )(x2d, g, w_in, cosf, sinf, *to_bf16)
    return outs[:len(widths)], outs[len(widths):]


def _sb_kernel(dq_ref, dk_ref, eq_ref, ek_ref, q_ref, k_ref, v_ref, tri_ref, o_ref,
               qh_ref, kh_ref, vh_ref, t_ref, pre_ref, tot_ref, w_ref, later_ref, acc_ref):
    seq = q_ref.shape[1]
    pairs = q_ref.shape[2] // LANES
    seq_tiles = seq // SB_TQ
    n_tiles = pairs * seq_tiles
    lane = lax.broadcasted_iota(jnp.int32, (SB_TQ, LANES), 1)
    head0 = lane < SB_HEAD_DIM
    row = lax.broadcasted_iota(jnp.int32, (SB_TQ, SB_TK), 0)
    col = lax.broadcasted_iota(jnp.int32, (SB_TQ, SB_TK), 1)
    rel = [col - row + c * SB_TK for c in range(SB_NB)]
    tri = tri_ref[...]

    def tile_rows(tile):
        return pl.ds(pl.multiple_of(tile * SB_TQ, SB_TQ), SB_TQ)

    for tile in range(n_tiles):
        src = (0, slice((tile % seq_tiles) * SB_TQ, (tile % seq_tiles + 1) * SB_TQ),
               slice((tile // seq_tiles) * LANES, (tile // seq_tiles + 1) * LANES))
        dst = slice(tile * SB_TQ, (tile + 1) * SB_TQ)
        q = q_ref[src]
        v = v_ref[src]
        zero = jnp.zeros_like(q)
        qh_ref[0, dst, :] = jnp.where(head0, q, zero)
        qh_ref[1, dst, :] = jnp.where(head0, zero, q)
        kh_ref[dst, :] = k_ref[src]
        vh_ref[tile, :SB_SPAN, :] = jnp.where(head0, v, zero)
        vh_ref[tile, SB_SPAN:, :] = jnp.where(head0, zero, v)
    w_ref[1] = jnp.zeros(w_ref.shape[1:], w_ref.dtype)

    def scores(hd, qt, kt):
        return lax.dot_general(qh_ref[hd, tile_rows(qt), :], kh_ref[tile_rows(kt), :],
                               (((1,), (1,)), ((), ())), preferred_element_type=F32)

    def suffix_sums(t, diagonal):
        sp = jnp.maximum(t, 0.0) + jnp.log2(1.0 + jnp.exp2(-jnp.abs(t)))
        if diagonal:
            sp = jnp.concatenate(
                [jnp.where(rel[c] < 0, sp[:, c * SB_TK:(c + 1) * SB_TK], 0.0)
                 for c in range(SB_NB)], axis=1)
        return jnp.dot(sp.astype(BF16), tri, preferred_element_type=F32)

    def stage_sums(t, sums, hd, slot):
        pre_ref[slot, hd] = t + sums
        tot_ref[slot, hd] = jnp.broadcast_to(sums[:, 0:1], (SB_TQ, SB_TK))

    def weights(hd, slot, qt, diagonal):
        run = None if diagonal else later_ref[qt, hd]
        for c in range(SB_NB):
            arg = pre_ref[slot, hd, :, c * SB_TK:(c + 1) * SB_TK]
            w = jnp.exp2(arg if run is None else arg + run)
            if diagonal:
                w = jnp.where(rel[c] < 0, w, 0.0)
            lo = hd * SB_SPAN + c * SB_TK
            w_ref[slot, :, lo:lo + SB_TK] = w.astype(BF16)
        tot = tot_ref[slot, hd]
        later_ref[qt, hd] = tot if run is None else run + tot

    def sweep(sq_ref, sk_ref, n_steps, diagonal):
        def step(s):
            return sq_ref[s + 1], sk_ref[s + 1]

        def weighted_values(s, slot):
            qt, kt = step(s)
            pv = jnp.dot(w_ref[slot], vh_ref[kt], preferred_element_type=F32)
            return qt, pv

        def accumulate(qt, pv):
            if diagonal:
                acc_ref[qt] = pv
            else:
                acc_ref[qt] += pv

        def half(s, slot):
            t = [t_ref[1 - slot, hd] for hd in range(2)]
            t_new = [scores(0, *step(s + 2))]
            sums = [suffix_sums(t[0], diagonal)]
            qt_prev, pv = weighted_values(s - 1, 1 - slot)
            t_new.append(scores(1, *step(s + 2)))
            sums.append(suffix_sums(t[1], diagonal))
            accumulate(qt_prev, pv)
            qt, _ = step(s)
            for hd in range(2):
                weights(hd, slot, qt, diagonal)
            for hd in range(2):
                t_ref[slot, hd] = t_new[hd]
            for hd in range(2):
                stage_sums(t[hd], sums[hd], hd, 1 - slot)

        for hd in range(2):
            t0 = scores(hd, *step(0))
            stage_sums(t0, suffix_sums(t0, diagonal), hd, 0)
        for hd in range(2):
            t_ref[1, hd] = scores(hd, *step(1))

        unroll = _sb_unroll(n_steps)

        def body(i, carry):
            for u in range(unroll):
                half(unroll * i + u, u % 2)
            return carry

        lax.fori_loop(0, n_steps // unroll, body, 0)
        accumulate(*weighted_values(n_steps - 1, 1))

    sweep(dq_ref, dk_ref, n_tiles, True)
    sweep(eq_ref, ek_ref, pairs * seq_tiles * (seq_tiles - 1) // 2, False)
    for tile in range(n_tiles):
        o_ref[0, (tile % seq_tiles) * SB_TQ:(tile % seq_tiles + 1) * SB_TQ,
              (tile // seq_tiles) * LANES:(tile // seq_tiles + 1) * LANES] = (
                  acc_ref[tile].astype(o_ref.dtype))


def _sb_unroll(n_steps):
    return max(u for u in range(2, SB_MAX_UNROLL + 1, 2) if n_steps % u == 0)


def _sb_steps(pairs, seq_tiles):
    n_tiles = pairs * seq_tiles
    diag = [(g, g) for g in range(n_tiles)]
    early = [(p * seq_tiles + t, p * seq_tiles + kt) for p in range(pairs)
             for t in range(1, seq_tiles) for kt in range(t - 1, -1, -1)]
    tables = []
    for steps in (diag, early):
        steps = [(n_tiles, 0)] + steps + [steps[-1]] * 2
        tables += [jnp.asarray([s[0] for s in steps], jnp.int32),
                   jnp.asarray([s[1] for s in steps], jnp.int32)]
    return tables


def _sb_attention(q, k, v, tri):
    b, seq, width = q.shape
    pairs = width // LANES
    seq_tiles = seq // SB_TQ
    n_tiles = pairs * seq_tiles
    spec = pl.BlockSpec((1, seq, width), lambda bi: (bi, 0, 0))
    smem = pl.BlockSpec(memory_space=pltpu.SMEM)
    return pl.pallas_call(
        _sb_kernel,
        grid=(b,),
        in_specs=[smem, smem, smem, smem, spec, spec, spec,
                  pl.BlockSpec((SB_SPAN, SB_SPAN), lambda bi: (0, 0))],
        out_specs=spec,
        out_shape=jax.ShapeDtypeStruct((b, seq, width), BF16),
        scratch_shapes=[pltpu.VMEM((2, n_tiles * SB_TQ, LANES), BF16),
                        pltpu.VMEM((n_tiles * SB_TQ, LANES), BF16),
                        pltpu.VMEM((n_tiles, 2 * SB_SPAN, LANES), BF16),
                        pltpu.VMEM((2, 2, SB_TQ, SB_SPAN), F32),
                        pltpu.VMEM((2, 2, SB_TQ, SB_SPAN), F32),
                        pltpu.VMEM((2, 2, SB_TQ, SB_TK), F32),
                        pltpu.VMEM((2, SB_TQ, 2 * SB_SPAN), BF16),
                        pltpu.VMEM((n_tiles, 2, SB_TQ, SB_TK), F32),
                        pltpu.VMEM((n_tiles + 1, SB_TQ, LANES), F32)],
        compiler_params=pltpu.CompilerParams(
            dimension_semantics=("arbitrary",), vmem_limit_bytes=VMEM_LIMIT),
        name="sb_attn",
    )(*_sb_steps(pairs, seq_tiles), q, k, v, tri)


def _ret_kernel(lg_ref, q_ref, k_ref, v_ref, g_ref, o_ref,
                decay_ref, qdec_ref, kdec_ref, state_ref):
    seq = q_ref.shape[1]
    c = RET_CHUNK
    heads = range(RET_HEADS)
    diff = (lax.broadcasted_iota(jnp.int32, (c, c), 0)
            - lax.broadcasted_iota(jnp.int32, (c, c), 1)).astype(F32)
    idx = lax.broadcasted_iota(jnp.int32, (c, LANES), 0).astype(F32)
    for h in heads:
        lg = lg_ref[h]
        decay_ref[h] = jnp.where(diff >= 0, jnp.exp(jnp.maximum(diff, 0.0) * lg), 0.0)
        qdec_ref[h] = jnp.exp((idx + 1.0) * lg)
        kdec_ref[h] = jnp.exp((c - 1.0 - idx) * lg)
    state_ref[...] = jnp.zeros(state_ref.shape, F32)

    def body(n, carry):
        sl = slice(n * c, (n + 1) * c)
        q, k, v = ([ref[0, sl, h * LANES:(h + 1) * LANES] for h in heads]
                   for ref in (q_ref, k_ref, v_ref))
        state = [state_ref[h] for h in heads]
        scores = [lax.dot_general(q[h], k[h], (((1,), (1,)), ((), ())),
                                  preferred_element_type=F32) for h in heads]
        cross = [jnp.dot((q[h].astype(F32) * qdec_ref[h]).astype(BF16),
                         state[h].astype(BF16), preferred_element_type=F32)
                 for h in heads]
        kd_t = [(k[h].astype(F32) * kdec_ref[h]).T.astype(BF16) for h in heads]
        kv = [jnp.dot(kd_t[h], v[h], preferred_element_type=F32) for h in heads]
        inner = [jnp.dot((scores[h] * decay_ref[h]).astype(BF16), v[h],
                         preferred_element_type=F32) for h in heads]
        for h in heads:
            chunk_decay = jnp.exp(jnp.full((1, LANES), float(c), F32) * lg_ref[h])
            state_ref[h] = chunk_decay * state[h] + kv[h]
        for h in heads:
            o = inner[h] + cross[h]
            mu = jnp.mean(o, axis=-1, keepdims=True)
            ctr = o - mu
            var = jnp.mean(ctr * ctr, axis=-1, keepdims=True)
            gate = g_ref[0, sl, h * LANES:(h + 1) * LANES]
            o_ref[0, sl, h * LANES:(h + 1) * LANES] = (
                ctr * lax.rsqrt(var + GN_EPS)
                * (gate * jax.nn.sigmoid(gate))).astype(o_ref.dtype)
        return carry

    for n in range(seq // c):
        body(n, 0)


def _retention(log_gamma, q, k, v, g):
    b, seq, width = q.shape
    spec = pl.BlockSpec((1, seq, width), lambda bi: (bi, 0, 0))
    c = RET_CHUNK
    return pl.pallas_call(
        _ret_kernel,
        grid=(b,),
        in_specs=[pl.BlockSpec(memory_space=pltpu.SMEM), spec, spec, spec, spec],
        out_specs=spec,
        out_shape=jax.ShapeDtypeStruct((b, seq, RET_WIDTH), BF16),
        scratch_shapes=[pltpu.VMEM((RET_HEADS, c, c), F32),
                        pltpu.VMEM((RET_HEADS, c, LANES), F32),
                        pltpu.VMEM((RET_HEADS, c, LANES), F32),
                        pltpu.VMEM((RET_HEADS, RET_KEY_DIM, LANES), F32)],
        compiler_params=pltpu.CompilerParams(
            dimension_semantics=("arbitrary",), vmem_limit_bytes=VMEM_LIMIT),
        name="retention",
    )(log_gamma, q, k, v, g)


def _merge_kernel(n_gate_blocks, x_ref, osb_ref, or_ref, gpre_ref, *refs):
    gate32_refs, refs = refs[:n_gate_blocks], refs[n_gate_blocks:]
    (wsb32_ref, wret32_ref, wout32_ref, gpost_ref, o_ref,
     wgate_ref, wsb_ref, wret_ref, wout_ref) = refs

    @pl.when(pl.program_id(0) == 0)
    def _():
        width = gate32_refs[0].shape[1]
        for b, ref in enumerate(gate32_refs):
            wgate_ref[:, b * width:(b + 1) * width] = ref[...].astype(BF16)
        wsb_ref[...] = wsb32_ref[...].astype(BF16)
        wret_ref[...] = wret32_ref[...].astype(BF16)
        wout_ref[...] = wout32_ref[...].astype(BF16)

    for r in range(MIX_TM // ROW_GROUP):
        rows = slice(r * ROW_GROUP, (r + 1) * ROW_GROUP)
        x = x_ref[rows, :]
        p_sb = jnp.dot(osb_ref[rows, :], wsb_ref[...], preferred_element_type=F32)
        p_ret = jnp.dot(or_ref[rows, :], wret_ref[...], preferred_element_type=F32)
        h = _rms(x, gpre_ref[...]).astype(BF16)
        gate_sb = jnp.dot(h, wgate_ref[:, :D_MODEL], preferred_element_type=F32)
        merged = jax.nn.sigmoid(gate_sb) * p_sb
        gate_ret = jnp.dot(h, wgate_ref[:, D_MODEL:], preferred_element_type=F32)
        merged = merged + jax.nn.sigmoid(gate_ret) * p_ret
        y = jnp.dot(merged.astype(BF16), wout_ref[...], preferred_element_type=F32)
        o_ref[rows, :] = x + _rms(y, gpost_ref[...])


def _merge(x2d, o_sb, o_r, g_pre, w_in, w_sb, w_ret, w_out, g_post):
    m = x2d.shape[0]
    gate_lo = w_in.shape[1] - 2 * D_MODEL
    gate_block = math.gcd(gate_lo, 2 * D_MODEL)
    n_gate_blocks = 2 * D_MODEL // gate_block

    def tile(width):
        return pl.BlockSpec((MIX_TM, width), lambda i: (i, 0))

    def gate_cols(b):
        return pl.BlockSpec((D_MODEL, gate_block),
                            lambda i: (0, gate_lo // gate_block + b),
                            pipeline_mode=pl.Buffered(1))

    gain = pl.BlockSpec((1, D_MODEL), lambda i: (0, 0))
    return pl.pallas_call(
        functools.partial(_merge_kernel, n_gate_blocks),
        grid=(m // MIX_TM,),
        in_specs=[tile(D_MODEL), tile(SB_WIDTH), tile(RET_WIDTH), gain]
                 + [gate_cols(b) for b in range(n_gate_blocks)]
                 + [_resident((SB_WIDTH, D_MODEL)), _resident((RET_WIDTH, D_MODEL)),
                    _resident((D_MODEL, D_MODEL)), gain],
        out_specs=tile(D_MODEL),
        out_shape=jax.ShapeDtypeStruct((m, D_MODEL), F32),
        scratch_shapes=[pltpu.VMEM((D_MODEL, 2 * D_MODEL), BF16),
                        pltpu.VMEM((SB_WIDTH, D_MODEL), BF16),
                        pltpu.VMEM((RET_WIDTH, D_MODEL), BF16),
                        pltpu.VMEM((D_MODEL, D_MODEL), BF16)],
        compiler_params=pltpu.CompilerParams(
            dimension_semantics=("arbitrary",), vmem_limit_bytes=VMEM_LIMIT),
        name="merge",
    )(x2d, o_sb, o_r, g_pre, *([w_in] * n_gate_blocks), w_sb, w_ret, w_out, g_post)


def _rope_tables(seq):
    half = RET_KEY_DIM // 2
    pos = np.arange(seq, dtype=np.float64)
    inv_freq = ROPE_BASE ** (-np.arange(half, dtype=np.float64) / half)
    ang = pos[:, None] * inv_freq[None, :]
    cos, sin = np.cos(ang), np.sin(ang)
    return (jnp.asarray(np.concatenate([cos, cos], axis=-1), dtype=F32),
            jnp.asarray(np.concatenate([-sin, sin], axis=-1), dtype=F32))


def _suffix_sum_matrix():
    j = np.arange(SB_SPAN)[:, None]
    s = np.arange(SB_SPAN)[None, :]
    return jnp.asarray(-(j >= s).astype(np.float32), dtype=BF16)


def kernel(x, g_ffn1_pre, g_ffn1_post, w_ffn1_gate, w_ffn1_up, w_ffn1_down,
           g_mix_pre, w_in, w_proj_sb, w_proj_ret, w_out, g_mix_post,
           g_ffn2_pre, g_ffn2_post, w_ffn2_gate, w_ffn2_up, w_ffn2_down):
    b, seq, d = x.shape
    depth = w_in.shape[0]
    log_gamma = jnp.log(1.0 - 2.0 ** (-5.0 - jnp.arange(RET_HEADS, dtype=F32)))
    cosf, sinf = _rope_tables(seq)
    tri = _suffix_sum_matrix()
    x2d = x.reshape(b * seq, d)
    for l in range(depth):
        x2d = _ffn(x2d, g_ffn1_pre[l][None], g_ffn1_post[l][None],
                   w_ffn1_gate[l].astype(BF16), w_ffn1_up[l].astype(BF16),
                   w_ffn1_down[l].astype(BF16))
        (q_sb, k_sb, v_sb, q_r, k_r, v_r, g_r), (w2_gate, w2_up, w2_down) = _mix_in(
            x2d, g_mix_pre[l][None], w_in[l], cosf, sinf, seq,
            to_bf16=(w_ffn2_gate[l], w_ffn2_up[l], w_ffn2_down[l]))
        shp = (b, seq, SB_WIDTH)
        o_sb = _sb_attention(q_sb.reshape(shp), k_sb.reshape(shp),
                             v_sb.reshape(shp), tri)
        o_r = _retention(log_gamma, q_r.reshape(shp), k_r.reshape(shp),
                         v_r.reshape(shp), g_r.reshape(shp))
        x2d = _merge(x2d, o_sb.reshape(b * seq, SB_WIDTH),
                     o_r.reshape(b * seq, RET_WIDTH),
                     g_mix_pre[l][None], w_in[l], w_proj_sb[l], w_proj_ret[l],
                     w_out[l], g_mix_post[l][None])
        x2d = _ffn(x2d, g_ffn2_pre[l][None], g_ffn2_post[l][None],
                   w2_gate, w2_up, w2_down)
    return x2d.reshape(b, seq, d)
```

```python
import functools
import math

import numpy as np
import jax
import jax.numpy as jnp
from jax import lax
from jax.experimental import pallas as pl
from jax.experimental.pallas import tpu as pltpu

F32 = jnp.float32
BF16 = jnp.bfloat16

D_MODEL = 1024
D_FF = 4 * D_MODEL
SB_HEADS = 8
SB_HEAD_DIM = 64
SB_WIDTH = SB_HEADS * SB_HEAD_DIM
RET_HEADS = 4
RET_KEY_DIM = 128
RET_WIDTH = RET_HEADS * RET_KEY_DIM
RMS_EPS = 1e-6
GN_EPS = 1e-5
ROPE_BASE = 10000.0

LANES = 128
BF16_SUBLANES = 16
VMEM_LIMIT = 56 * 1024 * 1024

FFN_TM = 1024
FFN_TF = 1024
MIX_IN_TM = 512
MIX_TM = 1024
ROW_GROUP = 256
SB_TQ = 256
SB_TK = LANES
SB_SPAN = SB_TQ
SB_NB = SB_SPAN // SB_TK
SB_MAX_UNROLL = 28
SB_Q_SCALE = SB_HEAD_DIM ** -0.5 * math.log2(math.e)
RET_CHUNK = 256


def _rms(x, g):
    ms = jnp.mean(x * x, axis=-1, keepdims=True)
    return x * lax.rsqrt(ms + RMS_EPS) * g


def _resident(shape):
    nd = len(shape)
    return pl.BlockSpec(shape, lambda *_: (0,) * nd, pipeline_mode=pl.Buffered(1))


def _cast_slabs(srcs, dsts):
    for src, dst in zip(srcs, dsts):
        dst[...] = src[...].astype(dst.dtype)


def _slab_specs(arrays, steps):
    assert all(w.shape[0] % (steps * BF16_SUBLANES) == 0 for w in arrays)
    return [pl.BlockSpec((w.shape[0] // steps, w.shape[1]), lambda i: (i, 0))
            for w in arrays]


def _ffn_kernel(x_ref, gpre_ref, gpost_ref, wg_ref, wu_ref, wd_ref, o_ref):
    groups = x_ref.shape[0] // ROW_GROUP
    rows = [slice(r * ROW_GROUP, (r + 1) * ROW_GROUP) for r in range(groups)]
    x = [x_ref[r, :] for r in rows]
    h = [_rms(xr, gpre_ref[...]).astype(BF16) for xr in x]
    acc = [None] * groups
    for j in range(D_FF // FFN_TF):
        cols = slice(j * FFN_TF, (j + 1) * FFN_TF)
        for r in range(groups):
            g = jnp.dot(h[r], wg_ref[:, cols], preferred_element_type=F32)
            u = jnp.dot(h[r], wu_ref[:, cols], preferred_element_type=F32)
            a = (g * jax.nn.sigmoid(g) * u).astype(BF16)
            d = jnp.dot(a, wd_ref[cols, :], preferred_element_type=F32)
            acc[r] = d if acc[r] is None else acc[r] + d
    for r in range(groups):
        o_ref[rows[r], :] = x[r] + 0.5 * _rms(acc[r], gpost_ref[...])


def _ffn(x2d, g_pre, g_post, wg, wu, wd):
    m = x2d.shape[0]
    tile = pl.BlockSpec((FFN_TM, D_MODEL), lambda i: (i, 0))
    gain = pl.BlockSpec((1, D_MODEL), lambda i: (0, 0))
    return pl.pallas_call(
        _ffn_kernel,
        grid=(m // FFN_TM,),
        in_specs=[tile, gain, gain,
                  _resident((D_MODEL, D_FF)), _resident((D_MODEL, D_FF)),
                  _resident((D_FF, D_MODEL))],
        out_specs=tile,
        out_shape=jax.ShapeDtypeStruct((m, D_MODEL), F32),
        compiler_params=pltpu.CompilerParams(
            dimension_semantics=("arbitrary",), vmem_limit_bytes=VMEM_LIMIT),
        name="ffn",
    )(x2d, g_pre, g_post, wg, wu, wd)

def _mix_in_kernel(n_casts, x_ref, g_ref, w32_ref, cos_ref, sin_ref, *refs):
    cast_in, refs = refs[:n_casts], refs[n_casts:]
    qsb_ref, ksb_ref, vsb_ref, qr_ref, kr_ref, vr_ref, gr_ref = refs[:7]
    cast_out, w_ref = refs[7:7 + n_casts], refs[7 + n_casts]

    @pl.when(pl.program_id(0) == 0)
    def _():
        w_ref[...] = w32_ref[...].astype(BF16)

    _cast_slabs(cast_in, cast_out)
    for r in range(MIX_IN_TM // ROW_GROUP):
        rows = slice(r * ROW_GROUP, (r + 1) * ROW_GROUP)
        h = _rms(x_ref[rows, :], g_ref[...]).astype(BF16)

        def mm(lo, width):
            return jnp.dot(h, w_ref[:, lo:lo + width], preferred_element_type=F32)

        def rotary(t):
            cosf = cos_ref[rows, :]
            sinf = sin_ref[rows, :]
            parts = []
            for hd in range(RET_HEADS):
                th = t[:, hd * LANES:(hd + 1) * LANES]
                parts.append(th * cosf + pltpu.roll(th, LANES // 2, axis=1) * sinf)
            return jnp.concatenate(parts, axis=1)

        c = 0
        qsb_ref[rows, :] = (mm(c, SB_WIDTH) * SB_Q_SCALE).astype(BF16); c += SB_WIDTH
        ksb_ref[rows, :] = mm(c, SB_WIDTH).astype(BF16); c += SB_WIDTH
        vsb_ref[rows, :] = mm(c, SB_WIDTH).astype(BF16); c += SB_WIDTH
        qr_ref[rows, :] = rotary(mm(c, RET_WIDTH)).astype(BF16); c += RET_WIDTH
        kr_ref[rows, :] = (rotary(mm(c, RET_WIDTH))
                           * (RET_KEY_DIM ** -0.5)).astype(BF16); c += RET_WIDTH
        vr_ref[rows, :] = mm(c, RET_WIDTH).astype(BF16); c += RET_WIDTH
        gr_ref[rows, :] = mm(c, RET_WIDTH)


def _mix_in(x2d, g, w_in, cosf, sinf, seq, to_bf16=()):
    m = x2d.shape[0]
    steps = m // MIX_IN_TM
    pos_blocks = seq // MIX_IN_TM

    def tile(width):
        return pl.BlockSpec((MIX_IN_TM, width), lambda i: (i, 0))

    rope = pl.BlockSpec((MIX_IN_TM, LANES), lambda i: (i % pos_blocks, 0))
    widths = [SB_WIDTH] * 3 + [RET_WIDTH] * 4
    dtypes = [BF16] * 6 + [F32]
    n_cols = sum(widths)
    slabs = _slab_specs(to_bf16, steps)
    outs = pl.pallas_call(
        functools.partial(_mix_in_kernel, len(to_bf16)),
        grid=(steps,),
        in_specs=[tile(D_MODEL), pl.BlockSpec((1, D_MODEL), lambda i: (0, 0)),
                  _resident((D_MODEL, n_cols)), rope, rope] + slabs,
        out_specs=[tile(w) for w in widths] + slabs,
        out_shape=[jax.ShapeDtypeStruct((m, w), dt) for w, dt in zip(widths, dtypes)]
                  + [jax.ShapeDtypeStruct(w.shape, BF16) for w in to_bf16],
        scratch_shapes=[pltpu.VMEM((D_MODEL, n_cols), BF16)],
        compiler_params=pltpu.CompilerParams(
            dimension_semantics=("arbitrary",), vmem_limit_bytes=VMEM_LIMIT),
        name="mix_in",
    )(x2d, g, w_in, cosf, sinf, *to_bf16)
    return outs[:len(widths)], outs[len(widths):]


def _sb_kernel(dq_ref, dk_ref, eq_ref, ek_ref, q_ref, k_ref, v_ref, tri_ref, o_ref,
               qh_ref, kh_ref, vh_ref, t_ref, pre_ref, tot_ref, w_ref, later_ref, acc_ref):
    seq = q_ref.shape[1]
    pairs = q_ref.shape[2] // LANES
    seq_tiles = seq // SB_TQ
    n_tiles = pairs * seq_tiles
    lane = lax.broadcasted_iota(jnp.int32, (SB_TQ, LANES), 1)
    head0 = lane < SB_HEAD_DIM
    row = lax.broadcasted_iota(jnp.int32, (SB_TQ, SB_TK), 0)
    col = lax.broadcasted_iota(jnp.int32, (SB_TQ, SB_TK), 1)
    rel = [col - row + c * SB_TK for c in range(SB_NB)]
    tri = tri_ref[...]

    def tile_rows(tile):
        return pl.ds(pl.multiple_of(tile * SB_TQ, SB_TQ), SB_TQ)

    for tile in range(n_tiles):
        src = (0, slice((tile % seq_tiles) * SB_TQ, (tile % seq_tiles + 1) * SB_TQ),
               slice((tile // seq_tiles) * LANES, (tile // seq_tiles + 1) * LANES))
        dst = slice(tile * SB_TQ, (tile + 1) * SB_TQ)
        q = q_ref[src]
        v = v_ref[src]
        zero = jnp.zeros_like(q)
        qh_ref[0, dst, :] = jnp.where(head0, q, zero)
        qh_ref[1, dst, :] = jnp.where(head0, zero, q)
        kh_ref[dst, :] = k_ref[src]
        vh_ref[tile, :SB_SPAN, :] = jnp.where(head0, v, zero)
        vh_ref[tile, SB_SPAN:, :] = jnp.where(head0, zero, v)
    w_ref[1] = jnp.zeros(w_ref.shape[1:], w_ref.dtype)

    def scores(hd, qt, kt):
        return lax.dot_general(qh_ref[hd, tile_rows(qt), :], kh_ref[tile_rows(kt), :],
                               (((1,), (1,)), ((), ())), preferred_element_type=F32)

    def suffix_sums(t, diagonal):
        sp = jnp.maximum(t, 0.0) + jnp.log2(1.0 + jnp.exp2(-jnp.abs(t)))
        if diagonal:
            sp = jnp.concatenate(
                [jnp.where(rel[c] < 0, sp[:, c * SB_TK:(c + 1) * SB_TK], 0.0)
                 for c in range(SB_NB)], axis=1)
        return jnp.dot(sp.astype(BF16), tri, preferred_element_type=F32)

    def stage_sums(t, sums, hd, slot):
        pre_ref[slot, hd] = t + sums
        tot_ref[slot, hd] = jnp.broadcast_to(sums[:, 0:1], (SB_TQ, SB_TK))

    def weights(hd, slot, qt, diagonal):
        run = None if diagonal else later_ref[qt, hd]
        for c in range(SB_NB):
            arg = pre_ref[slot, hd, :, c * SB_TK:(c + 1) * SB_TK]
            w = jnp.exp2(arg if run is None else arg + run)
            if diagonal:
                w = jnp.where(rel[c] < 0, w, 0.0)
            lo = hd * SB_SPAN + c * SB_TK
            w_ref[slot, :, lo:lo + SB_TK] = w.astype(BF16)
        tot = tot_ref[slot, hd]
        later_ref[qt, hd] = tot if run is None else run + tot

    def sweep(sq_ref, sk_ref, n_steps, diagonal):
        def step(s):
            return sq_ref[s + 1], sk_ref[s + 1]

        def weighted_values(s, slot):
            qt, kt = step(s)
            pv = jnp.dot(w_ref[slot], vh_ref[kt], preferred_element_type=F32)
            return qt, pv

        def accumulate(qt, pv):
            if diagonal:
                acc_ref[qt] = pv
            else:
                acc_ref[qt] += pv

        def half(s, slot):
            t = [t_ref[1 - slot, hd] for hd in range(2)]
            t_new = [scores(0, *step(s + 2))]
            sums = [suffix_sums(t[0], diagonal)]
            qt_prev, pv = weighted_values(s - 1, 1 - slot)
            t_new.append(scores(1, *step(s + 2)))
            sums.append(suffix_sums(t[1], diagonal))
            accumulate(qt_prev, pv)
            qt, _ = step(s)
            for hd in range(2):
                weights(hd, slot, qt, diagonal)
            for hd in range(2):
                t_ref[slot, hd] = t_new[hd]
            for hd in range(2):
                stage_sums(t[hd], sums[hd], hd, 1 - slot)

        for hd in range(2):
            t0 = scores(hd, *step(0))
            stage_sums(t0, suffix_sums(t0, diagonal), hd, 0)
        for hd in range(2):
            t_ref[1, hd] = scores(hd, *step(1))

        unroll = _sb_unroll(n_steps)

        def body(i, carry):
            for u in range(unroll):
                half(unroll * i + u, u % 2)
            return carry

        lax.fori_loop(0, n_steps // unroll, body, 0)
        accumulate(*weighted_values(n_steps - 1, 1))

    sweep(dq_ref, dk_ref, n_tiles, True)
    sweep(eq_ref, ek_ref, pairs * seq_tiles * (seq_tiles - 1) // 2, False)
    for tile in range(n_tiles):
        o_ref[0, (tile % seq_tiles) * SB_TQ:(tile % seq_tiles + 1) * SB_TQ,
              (tile // seq_tiles) * LANES:(tile // seq_tiles + 1) * LANES] = (
                  acc_ref[tile].astype(o_ref.dtype))


def _sb_unroll(n_steps):
    return max(u for u in range(2, SB_MAX_UNROLL + 1, 2) if n_steps % u == 0)


def _sb_steps(pairs, seq_tiles):
    n_tiles = pairs * seq_tiles
    diag = [(g, g) for g in range(n_tiles)]
    early = [(p * seq_tiles + t, p * seq_tiles + kt) for p in range(pairs)
             for t in range(1, seq_tiles) for kt in range(t - 1, -1, -1)]
    tables = []
    for steps in (diag, early):
        steps = [(n_tiles, 0)] + steps + [steps[-1]] * 2
        tables += [jnp.asarray([s[0] for s in steps], jnp.int32),
                   jnp.asarray([s[1] for s in steps], jnp.int32)]
    return tables


def _sb_attention(q, k, v, tri):
    b, seq, width = q.shape
    pairs = width // LANES
    seq_tiles = seq // SB_TQ
    n_tiles = pairs * seq_tiles
    spec = pl.BlockSpec((1, seq, width), lambda bi: (bi, 0, 0))
    smem = pl.BlockSpec(memory_space=pltpu.SMEM)
    return pl.pallas_call(
        _sb_kernel,
        grid=(b,),
        in_specs=[smem, smem, smem, smem, spec, spec, spec,
                  pl.BlockSpec((SB_SPAN, SB_SPAN), lambda bi: (0, 0))],
        out_specs=spec,
        out_shape=jax.ShapeDtypeStruct((b, seq, width), BF16),
        scratch_shapes=[pltpu.VMEM((2, n_tiles * SB_TQ, LANES), BF16),
                        pltpu.VMEM((n_tiles * SB_TQ, LANES), BF16),
                        pltpu.VMEM((n_tiles, 2 * SB_SPAN, LANES), BF16),
                        pltpu.VMEM((2, 2, SB_TQ, SB_SPAN), F32),
                        pltpu.VMEM((2, 2, SB_TQ, SB_SPAN), F32),
                        pltpu.VMEM((2, 2, SB_TQ, SB_TK), F32),
                        pltpu.VMEM((2, SB_TQ, 2 * SB_SPAN), BF16),
                        pltpu.VMEM((n_tiles, 2, SB_TQ, SB_TK), F32),
                        pltpu.VMEM((n_tiles + 1, SB_TQ, LANES), F32)],
        compiler_params=pltpu.CompilerParams(
            dimension_semantics=("arbitrary",), vmem_limit_bytes=VMEM_LIMIT),
        name="sb_attn",
    )(*_sb_steps(pairs, seq_tiles), q, k, v, tri)


def _ret_kernel(lg_ref, q_ref, k_ref, v_ref, g_ref, o_ref,
                decay_ref, qdec_ref, kdec_ref, state_ref):
    seq = q_ref.shape[1]
    c = RET_CHUNK
    heads = range(RET_HEADS)
    diff = (lax.broadcasted_iota(jnp.int32, (c, c), 0)
            - lax.broadcasted_iota(jnp.int32, (c, c), 1)).astype(F32)
    idx = lax.broadcasted_iota(jnp.int32, (c, LANES), 0).astype(F32)
    for h in heads:
        lg = lg_ref[h]
        decay_ref[h] = jnp.where(diff >= 0, jnp.exp(jnp.maximum(diff, 0.0) * lg), 0.0)
        qdec_ref[h] = jnp.exp((idx + 1.0) * lg)
        kdec_ref[h] = jnp.exp((c - 1.0 - idx) * lg)
    state_ref[...] = jnp.zeros(state_ref.shape, F32)

    def body(n, carry):
        sl = slice(n * c, (n + 1) * c)
        q, k, v = ([ref[0, sl, h * LANES:(h + 1) * LANES] for h in heads]
                   for ref in (q_ref, k_ref, v_ref))
        state = [state_ref[h] for h in heads]
        scores = [lax.dot_general(q[h], k[h], (((1,), (1,)), ((), ())),
                                  preferred_element_type=F32) for h in heads]
        cross = [jnp.dot((q[h].astype(F32) * qdec_ref[h]).astype(BF16),
                         state[h].astype(BF16), preferred_element_type=F32)
                 for h in heads]
        kd_t = [(k[h].astype(F32) * kdec_ref[h]).T.astype(BF16) for h in heads]
        kv = [jnp.dot(kd_t[h], v[h], preferred_element_type=F32) for h in heads]
        inner = [jnp.dot((scores[h] * decay_ref[h]).astype(BF16), v[h],
                         preferred_element_type=F32) for h in heads]
        for h in heads:
            chunk_decay = jnp.exp(jnp.full((1, LANES), float(c), F32) * lg_ref[h])
            state_ref[h] = chunk_decay * state[h] + kv[h]
        for h in heads:
            o = inner[h] + cross[h]
            mu = jnp.mean(o, axis=-1, keepdims=True)
            ctr = o - mu
            var = jnp.mean(ctr * ctr, axis=-1, keepdims=True)
            gate = g_ref[0, sl, h * LANES:(h + 1) * LANES]
            o_ref[0, sl, h * LANES:(h + 1) * LANES] = (
                ctr * lax.rsqrt(var + GN_EPS)
                * (gate * jax.nn.sigmoid(gate))).astype(o_ref.dtype)
        return carry

    for n in range(seq // c):
        body(n, 0)


def _retention(log_gamma, q, k, v, g):
    b, seq, width = q.shape
    spec = pl.BlockSpec((1, seq, width), lambda bi: (bi, 0, 0))
    c = RET_CHUNK
    return pl.pallas_call(
        _ret_kernel,
        grid=(b,),
        in_specs=[pl.BlockSpec(memory_space=pltpu.SMEM), spec, spec, spec, spec],
        out_specs=spec,
        out_shape=jax.ShapeDtypeStruct((b, seq, RET_WIDTH), BF16),
        scratch_shapes=[pltpu.VMEM((RET_HEADS, c, c), F32),
                        pltpu.VMEM((RET_HEADS, c, LANES), F32),
                        pltpu.VMEM((RET_HEADS, c, LANES), F32),
                        pltpu.VMEM((RET_HEADS, RET_KEY_DIM, LANES), F32)],
        compiler_params=pltpu.CompilerParams(
            dimension_semantics=("arbitrary",), vmem_limit_bytes=VMEM_LIMIT),
        name="retention",
    )(log_gamma, q, k, v, g)


def _merge_kernel(n_gate_blocks, x_ref, osb_ref, or_ref, gpre_ref, *refs):
    gate32_refs, refs = refs[:n_gate_blocks], refs[n_gate_blocks:]
    (wsb32_ref, wret32_ref, wout32_ref, gpost_ref, o_ref,
     wgate_ref, wsb_ref, wret_ref, wout_ref) = refs

    @pl.when(pl.program_id(0) == 0)
    def _():
        width = gate32_refs[0].shape[1]
        for b, ref in enumerate(gate32_refs):
            wgate_ref[:, b * width:(b + 1) * width] = ref[...].astype(BF16)
        wsb_ref[...] = wsb32_ref[...].astype(BF16)
        wret_ref[...] = wret32_ref[...].astype(BF16)
        wout_ref[...] = wout32_ref[...].astype(BF16)

    for r in range(MIX_TM // ROW_GROUP):
        rows = slice(r * ROW_GROUP, (r + 1) * ROW_GROUP)
        x = x_ref[rows, :]
        p_sb = jnp.dot(osb_ref[rows, :], wsb_ref[...], preferred_element_type=F32)
        p_ret = jnp.dot(or_ref[rows, :], wret_ref[...], preferred_element_type=F32)
        h = _rms(x, gpre_ref[...]).astype(BF16)
        gate_sb = jnp.dot(h, wgate_ref[:, :D_MODEL], preferred_element_type=F32)
        merged = jax.nn.sigmoid(gate_sb) * p_sb
        gate_ret = jnp.dot(h, wgate_ref[:, D_MODEL:], preferred_element_type=F32)
        merged = merged + jax.nn.sigmoid(gate_ret) * p_ret
        y = jnp.dot(merged.astype(BF16), wout_ref[...], preferred_element_type=F32)
        o_ref[rows, :] = x + _rms(y, gpost_ref[...])


def _merge(x2d, o_sb, o_r, g_pre, w_in, w_sb, w_ret, w_out, g_post):
    m = x2d.shape[0]
    gate_lo = w_in.shape[1] - 2 * D_MODEL
    gate_block = math.gcd(gate_lo, 2 * D_MODEL)
    n_gate_blocks = 2 * D_MODEL // gate_block

    def tile(width):
        return pl.BlockSpec((MIX_TM, width), lambda i: (i, 0))

    def gate_cols(b):
        return pl.BlockSpec((D_MODEL, gate_block),
                            lambda i: (0, gate_lo // gate_block + b),
                            pipeline_mode=pl.Buffered(1))

    gain = pl.BlockSpec((1, D_MODEL), lambda i: (0, 0))
    return pl.pallas_call(
        functools.partial(_merge_kernel, n_gate_blocks),
        grid=(m // MIX_TM,),
        in_specs=[tile(D_MODEL), tile(SB_WIDTH), tile(RET_WIDTH), gain]
                 + [gate_cols(b) for b in range(n_gate_blocks)]
                 + [_resident((SB_WIDTH, D_MODEL)), _resident((RET_WIDTH, D_MODEL)),
                    _resident((D_MODEL, D_MODEL)), gain],
        out_specs=tile(D_MODEL),
        out_shape=jax.ShapeDtypeStruct((m, D_MODEL), F32),
        scratch_shapes=[pltpu.VMEM((D_MODEL, 2 * D_MODEL), BF16),
                        pltpu.VMEM((SB_WIDTH, D_MODEL), BF16),
                        pltpu.VMEM((RET_WIDTH, D_MODEL), BF16),
                        pltpu.VMEM((D_MODEL, D_MODEL), BF16)],
        compiler_params=pltpu.CompilerParams(
            dimension_semantics=("arbitrary",), vmem_limit_bytes=VMEM_LIMIT),
        name="merge",
    )(x2d, o_sb, o_r, g_pre, *([w_in] * n_gate_blocks), w_sb, w_ret, w_out, g_post)


def _rope_tables(seq):
    half = RET_KEY_DIM // 2
    pos = np.arange(seq, dtype=np.float64)
    inv_freq = ROPE_BASE ** (-np.arange(half, dtype=np.float64) / half)
    ang = pos[:, None] * inv_freq[None, :]
    cos, sin = np.cos(ang), np.sin(ang)
    return (jnp.asarray(np.concatenate([cos, cos], axis=-1), dtype=F32),
            jnp.asarray(np.concatenate([-sin, sin], axis=-1), dtype=F32))


def _suffix_sum_matrix():
    j = np.arange(SB_SPAN)[:, None]
    s = np.arange(SB_SPAN)[None, :]
    return jnp.asarray(-(j >= s).astype(np.float32), dtype=BF16)


def kernel(x, g_ffn1_pre, g_ffn1_post, w_ffn1_gate, w_ffn1_up, w_ffn1_down,
           g_mix_pre, w_in, w_proj_sb, w_proj_ret, w_out, g_mix_post,
           g_ffn2_pre, g_ffn2_post, w_ffn2_gate, w_ffn2_up, w_ffn2_down):
    b, seq, d = x.shape
    depth = w_in.shape[0]
    log_gamma = jnp.log(1.0 - 2.0 ** (-5.0 - jnp.arange(RET_HEADS, dtype=F32)))
    cosf, sinf = _rope_tables(seq)
    tri = _suffix_sum_matrix()
    x2d = x.reshape(b * seq, d)
    for l in range(depth):
        x2d = _ffn(x2d, g_ffn1_pre[l][None], g_ffn1_post[l][None],
                   w_ffn1_gate[l].astype(BF16), w_ffn1_up[l].astype(BF16),
                   w_ffn1_down[l].astype(BF16))
        (q_sb, k_sb, v_sb, q_r, k_r, v_r, g_r), (w2_gate, w2_up, w2_down) = _mix_in(
            x2d, g_mix_pre[l][None], w_in[l], cosf, sinf, seq,
            to_bf16=(w_ffn2_gate[l], w_ffn2_up[l], w_ffn2_down[l]))
        shp = (b, seq, SB_WIDTH)
        o_sb = _sb_attention(q_sb.reshape(shp), k_sb.reshape(shp),
                             v_sb.reshape(shp), tri)
        o_r = _retention(log_gamma, q_r.reshape(shp), k_r.reshape(shp),
                         v_r.reshape(shp), g_r.reshape(shp))
        x2d = _merge(x2d, o_sb.reshape(b * seq, SB_WIDTH),
                     o_r.reshape(b * seq, RET_WIDTH),
                     g_mix_pre[l][None], w_in[l], w_proj_sb[l], w_proj_ret[l],
                     w_out[l], g_mix_post[l][None])
        x2d = _ffn(x2d, g_ffn2_pre[l][None], g_ffn2_post[l][None],
                   w2_gate, w2_up, w2_down)
    return x2d.reshape(b, seq, d)
```

```python
import functools
import math

import numpy as np
import jax
import jax.numpy as jnp
from jax import lax
from jax.experimental import pallas as pl
from jax.experimental.pallas import tpu as pltpu

F32 = jnp.float32
BF16 = jnp.bfloat16

D_MODEL = 1024
D_FF = 4 * D_MODEL
SB_HEADS = 8
SB_HEAD_DIM = 64
SB_WIDTH = SB_HEADS * SB_HEAD_DIM
RET_HEADS = 4
RET_KEY_DIM = 128
RET_WIDTH = RET_HEADS * RET_KEY_DIM
RMS_EPS = 1e-6
GN_EPS = 1e-5
ROPE_BASE = 10000.0

LANES = 128
BF16_SUBLANES = 16
VMEM_LIMIT = 56 * 1024 * 1024

FFN_TM = 1024
FFN_TF = 1024
MIX_IN_TM = 512
MIX_TM = 1024
ROW_GROUP = 256
SB_TQ = 256
SB_TK = LANES
SB_SPAN = SB_TQ
SB_NB = SB_SPAN // SB_TK
SB_MAX_UNROLL = 28
SB_Q_SCALE = SB_HEAD_DIM ** -0.5 * math.log2(math.e)
RET_CHUNK = 256


def _rms(x, g):
    ms = jnp.mean(x * x, axis=-1, keepdims=True)
    return x * lax.rsqrt(ms + RMS_EPS) * g


def _resident(shape):
    nd = len(shape)
    return pl.BlockSpec(shape, lambda *_: (0,) * nd, pipeline_mode=pl.Buffered(1))


def _cast_slabs(srcs, dsts):
    for src, dst in zip(srcs, dsts):
        dst[...] = src[...].astype(dst.dtype)


def _slab_specs(arrays, steps):
    assert all(w.shape[0] % (steps * BF16_SUBLANES) == 0 for w in arrays)
    return [pl.BlockSpec((w.shape[0] // steps, w.shape[1]), lambda i: (i, 0))
            for w in arrays]


def _ffn_kernel(x_ref, gpre_ref, gpost_ref, wg_ref, wu_ref, wd_ref, o_ref):
    groups = x_ref.shape[0] // ROW_GROUP
    rows = [slice(r * ROW_GROUP, (r + 1) * ROW_GROUP) for r in range(groups)]
    x = [x_ref[r, :] for r in rows]
    h = [_rms(xr, gpre_ref[...]).astype(BF16) for xr in x]
    acc = [None] * groups
    for j in range(D_FF // FFN_TF):
        cols = slice(j * FFN_TF, (j + 1) * FFN_TF)
        for r in range(groups):
            g = jnp.dot(h[r], wg_ref[:, cols], preferred_element_type=F32)
            u = jnp.dot(h[r], wu_ref[:, cols], preferred_element_type=F32)
            a = (g * jax.nn.sigmoid(g) * u).astype(BF16)
            d = jnp.dot(a, wd_ref[cols, :], preferred_element_type=F32)
            acc[r] = d if acc[r] is None else acc[r] + d
    for r in range(groups):
        o_ref[rows[r], :] = x[r] + 0.5 * _rms(acc[r], gpost_ref[...])


def _ffn(x2d, g_pre, g_post, wg, wu, wd):
    m = x2d.shape[0]
    tile = pl.BlockSpec((FFN_TM, D_MODEL), lambda i: (i, 0))
    gain = pl.BlockSpec((1, D_MODEL), lambda i: (0, 0))
    return pl.pallas_call(
        _ffn_kernel,
        grid=(m // FFN_TM,),
        in_specs=[tile, gain, gain,
                  _resident((D_MODEL, D_FF)), _resident((D_MODEL, D_FF)),
                  _resident((D_FF, D_MODEL))],
        out_specs=tile,
        out_shape=jax.ShapeDtypeStruct((m, D_MODEL), F32),
        compiler_params=pltpu.CompilerParams(
            dimension_semantics=("arbitrary",), vmem_limit_bytes=VMEM_LIMIT),
        name="ffn",
    )(x2d, g_pre, g_post, wg, wu, wd)

def _mix_in_kernel(n_casts, x_ref, g_ref, w32_ref, cos_ref, sin_ref, *refs):
    cast_in, refs = refs[:n_casts], refs[n_casts:]
    qsb_ref, ksb_ref, vsb_ref, qr_ref, kr_ref, vr_ref, gr_ref = refs[:7]
    cast_out, w_ref = refs[7:7 + n_casts], refs[7 + n_casts]

    @pl.when(pl.program_id(0) == 0)
    def _():
        w_ref[...] = w32_ref[...].astype(BF16)

    _cast_slabs(cast_in, cast_out)
    for r in range(MIX_IN_TM // ROW_GROUP):
        rows = slice(r * ROW_GROUP, (r + 1) * ROW_GROUP)
        h = _rms(x_ref[rows, :], g_ref[...]).astype(BF16)

        def mm(lo, width):
            return jnp.dot(h, w_ref[:, lo:lo + width], preferred_element_type=F32)

        def rotary(t):
            cosf = cos_ref[rows, :]
            sinf = sin_ref[rows, :]
            parts = []
            for hd in range(RET_HEADS):
                th = t[:, hd * LANES:(hd + 1) * LANES]
                parts.append(th * cosf + pltpu.roll(th, LANES // 2, axis=1) * sinf)
            return jnp.concatenate(parts, axis=1)

        c = 0
        qsb_ref[rows, :] = (mm(c, SB_WIDTH) * SB_Q_SCALE).astype(BF16); c += SB_WIDTH
        ksb_ref[rows, :] = mm(c, SB_WIDTH).astype(BF16); c += SB_WIDTH
        vsb_ref[rows, :] = mm(c, SB_WIDTH).astype(BF16); c += SB_WIDTH
        qr_ref[rows, :] = rotary(mm(c, RET_WIDTH)).astype(BF16); c += RET_WIDTH
        kr_ref[rows, :] = (rotary(mm(c, RET_WIDTH))
                           * (RET_KEY_DIM ** -0.5)).astype(BF16); c += RET_WIDTH
        vr_ref[rows, :] = mm(c, RET_WIDTH).astype(BF16); c += RET_WIDTH
        gr_ref[rows, :] = mm(c, RET_WIDTH)


def _mix_in(x2d, g, w_in, cosf, sinf, seq, to_bf16=()):
    m = x2d.shape[0]
    steps = m // MIX_IN_TM
    pos_blocks = seq // MIX_IN_TM

    def tile(width):
        return pl.BlockSpec((MIX_IN_TM, width), lambda i: (i, 0))

    rope = pl.BlockSpec((MIX_IN_TM, LANES), lambda i: (i % pos_blocks, 0))
    widths = [SB_WIDTH] * 3 + [RET_WIDTH] * 4
    dtypes = [BF16] * 6 + [F32]
    n_cols = sum(widths)
    slabs = _slab_specs(to_bf16, steps)
    outs = pl.pallas_call(
        functools.partial(_mix_in_kernel, len(to_bf16)),
        grid=(steps,),
        in_specs=[tile(D_MODEL), pl.BlockSpec((1, D_MODEL), lambda i: (0, 0)),
                  _resident((D_MODEL, n_cols)), rope, rope] + slabs,
        out_specs=[tile(w) for w in widths] + slabs,
        out_shape=[jax.ShapeDtypeStruct((m, w), dt) for w, dt in zip(widths, dtypes)]
                  + [jax.ShapeDtypeStruct(w.shape, BF16) for w in to_bf16],
        scratch_shapes=[pltpu.VMEM((D_MODEL, n_cols), BF16)],
        compiler_params=pltpu.CompilerParams(
            dimension_semantics=("arbitrary",), vmem_limit_bytes=VMEM_LIMIT),
        name="mix_in",
    )(x2d, g, w_in, cosf, sinf, *to_bf16)
    return outs[:len(widths)], outs[len(widths):]


def _sb_kernel(dq_ref, dk_ref, eq_ref, ek_ref, q_ref, k_ref, v_ref, tri_ref, o_ref,
               qh_ref, kh_ref, vh_ref, t_ref, lsig_ref, keep_ref, tot_ref, w_ref,
               later_ref, acc_ref):
    seq = q_ref.shape[1]
    pairs = q_ref.shape[2] // LANES
    seq_tiles = seq // SB_TQ
    n_tiles = pairs * seq_tiles
    lane = lax.broadcasted_iota(jnp.int32, (SB_TQ, LANES), 1)
    head0 = lane < SB_HEAD_DIM
    row = lax.broadcasted_iota(jnp.int32, (SB_TQ, SB_TK), 0)
    col = lax.broadcasted_iota(jnp.int32, (SB_TQ, SB_TK), 1)
    rel = [col - row + c * SB_TK for c in range(SB_NB)]
    tri = tri_ref[...]

    def tile_rows(tile):
        return pl.ds(pl.multiple_of(tile * SB_TQ, SB_TQ), SB_TQ)

    for tile in range(n_tiles):
        src = (0, slice((tile % seq_tiles) * SB_TQ, (tile % seq_tiles + 1) * SB_TQ),
               slice((tile // seq_tiles) * LANES, (tile // seq_tiles + 1) * LANES))
        dst = slice(tile * SB_TQ, (tile + 1) * SB_TQ)
        q = q_ref[src]
        v = v_ref[src]
        zero = jnp.zeros_like(q)
        qh_ref[0, dst, :] = jnp.where(head0, q, zero)
        qh_ref[1, dst, :] = jnp.where(head0, zero, q)
        kh_ref[dst, :] = k_ref[src]
        vh_ref[tile, :SB_SPAN, :] = jnp.where(head0, v, zero)
        vh_ref[tile, SB_SPAN:, :] = jnp.where(head0, zero, v)
    w_ref[1] = jnp.zeros(w_ref.shape[1:], w_ref.dtype)

    def scores(hd, qt, kt):
        return lax.dot_general(qh_ref[hd, tile_rows(qt), :], kh_ref[tile_rows(kt), :],
                               (((1,), (1,)), ((), ())), preferred_element_type=F32)

    def suffix_sums(t, hd, slot, diagonal):
        sp = jnp.maximum(t, 0.0) + jnp.log2(1.0 + jnp.exp2(-jnp.abs(t)))
        lsig_ref[slot, hd] = t - sp
        if diagonal:
            sp = jnp.concatenate(
                [jnp.where(rel[c] < 0, sp[:, c * SB_TK:(c + 1) * SB_TK], 0.0)
                 for c in range(SB_NB)], axis=1)
        return sp[:, 0:1], jnp.dot(sp.astype(BF16), tri, preferred_element_type=F32)

    def stage_sums(parts, hd, slot):
        sp_first, sums = parts
        keep_ref[slot, hd] = sums
        tot_ref[slot, hd] = jnp.broadcast_to(sums[:, 0:1] - sp_first, (SB_TQ, SB_TK))

    def weights(hd, slot, qt, diagonal):
        run = None if diagonal else later_ref[qt, hd]
        for c in range(SB_NB):
            cols = slice(c * SB_TK, (c + 1) * SB_TK)
            arg = lsig_ref[slot, hd, :, cols] + keep_ref[slot, hd, :, cols]
            w = jnp.exp2(arg if run is None else arg + run)
            if diagonal:
                w = jnp.where(rel[c] < 0, w, 0.0)
            lo = hd * SB_SPAN + c * SB_TK
            w_ref[slot, :, lo:lo + SB_TK] = w.astype(BF16)
        tot = tot_ref[slot, hd]
        later_ref[qt, hd] = tot if run is None else run + tot

    def sweep(sq_ref, sk_ref, n_steps, diagonal):
        def step(s):
            return sq_ref[s + 1], sk_ref[s + 1]

        def weighted_values(s, slot):
            qt, kt = step(s)
            pv = jnp.dot(w_ref[slot], vh_ref[kt], preferred_element_type=F32)
            return qt, pv

        def accumulate(qt, pv):
            if diagonal:
                acc_ref[qt] = pv
            else:
                acc_ref[qt] += pv

        def half(s, slot):
            t = [t_ref[1 - slot, hd] for hd in range(2)]
            t_new = [scores(0, *step(s + 2))]
            sums = [suffix_sums(t[0], 0, 1 - slot, diagonal)]
            qt_prev, pv = weighted_values(s - 1, 1 - slot)
            t_new.append(scores(1, *step(s + 2)))
            sums.append(suffix_sums(t[1], 1, 1 - slot, diagonal))
            accumulate(qt_prev, pv)
            qt, _ = step(s)
            for hd in range(2):
                weights(hd, slot, qt, diagonal)
            for hd in range(2):
                t_ref[slot, hd] = t_new[hd]
            for hd in range(2):
                stage_sums(sums[hd], hd, 1 - slot)

        for hd in range(2):
            stage_sums(suffix_sums(scores(hd, *step(0)), hd, 0, diagonal), hd, 0)
        for hd in range(2):
            t_ref[1, hd] = scores(hd, *step(1))

        unroll = _sb_unroll(n_steps)

        def body(i, carry):
            for u in range(unroll):
                half(unroll * i + u, u % 2)
            return carry

        lax.fori_loop(0, n_steps // unroll, body, 0)
        accumulate(*weighted_values(n_steps - 1, 1))

    sweep(dq_ref, dk_ref, n_tiles, True)
    sweep(eq_ref, ek_ref, pairs * seq_tiles * (seq_tiles - 1) // 2, False)
    for tile in range(n_tiles):
        o_ref[0, (tile % seq_tiles) * SB_TQ:(tile % seq_tiles + 1) * SB_TQ,
              (tile // seq_tiles) * LANES:(tile // seq_tiles + 1) * LANES] = (
                  acc_ref[tile].astype(o_ref.dtype))


def _sb_unroll(n_steps):
    return max(u for u in range(2, SB_MAX_UNROLL + 1, 2) if n_steps % u == 0)


def _sb_steps(pairs, seq_tiles):
    n_tiles = pairs * seq_tiles
    diag = [(g, g) for g in range(n_tiles)]
    early = [(p * seq_tiles + t, p * seq_tiles + kt) for p in range(pairs)
             for t in range(1, seq_tiles) for kt in range(t - 1, -1, -1)]
    tables = []
    for steps in (diag, early):
        steps = [(n_tiles, 0)] + steps + [steps[-1]] * 2
        tables += [jnp.asarray([s[0] for s in steps], jnp.int32),
                   jnp.asarray([s[1] for s in steps], jnp.int32)]
    return tables


def _sb_attention(q, k, v, tri):
    b, seq, width = q.shape
    pairs = width // LANES
    seq_tiles = seq // SB_TQ
    n_tiles = pairs * seq_tiles
    spec = pl.BlockSpec((1, seq, width), lambda bi: (bi, 0, 0))
    smem = pl.BlockSpec(memory_space=pltpu.SMEM)
    return pl.pallas_call(
        _sb_kernel,
        grid=(b,),
        in_specs=[smem, smem, smem, smem, spec, spec, spec,
                  pl.BlockSpec((SB_SPAN, SB_SPAN), lambda bi: (0, 0))],
        out_specs=spec,
        out_shape=jax.ShapeDtypeStruct((b, seq, width), BF16),
        scratch_shapes=[pltpu.VMEM((2, n_tiles * SB_TQ, LANES), BF16),
                        pltpu.VMEM((n_tiles * SB_TQ, LANES), BF16),
                        pltpu.VMEM((n_tiles, 2 * SB_SPAN, LANES), BF16),
                        pltpu.VMEM((2, 2, SB_TQ, SB_SPAN), F32),
                        pltpu.VMEM((2, 2, SB_TQ, SB_SPAN), F32),
                        pltpu.VMEM((2, 2, SB_TQ, SB_SPAN), F32),
                        pltpu.VMEM((2, 2, SB_TQ, SB_TK), F32),
                        pltpu.VMEM((2, SB_TQ, 2 * SB_SPAN), BF16),
                        pltpu.VMEM((n_tiles, 2, SB_TQ, SB_TK), F32),
                        pltpu.VMEM((n_tiles + 1, SB_TQ, LANES), F32)],
        compiler_params=pltpu.CompilerParams(
            dimension_semantics=("arbitrary",), vmem_limit_bytes=VMEM_LIMIT),
        name="sb_attn",
    )(*_sb_steps(pairs, seq_tiles), q, k, v, tri)


def _ret_kernel(lg_ref, q_ref, k_ref, v_ref, g_ref, o_ref,
                decay_ref, qdec_ref, kdec_ref, state_ref):
    seq = q_ref.shape[1]
    c = RET_CHUNK
    heads = range(RET_HEADS)
    diff = (lax.broadcasted_iota(jnp.int32, (c, c), 0)
            - lax.broadcasted_iota(jnp.int32, (c, c), 1)).astype(F32)
    idx = lax.broadcasted_iota(jnp.int32, (c, LANES), 0).astype(F32)
    for h in heads:
        lg = lg_ref[h]
        decay_ref[h] = jnp.where(diff >= 0, jnp.exp(jnp.maximum(diff, 0.0) * lg), 0.0)
        qdec_ref[h] = jnp.exp((idx + 1.0) * lg)
        kdec_ref[h] = jnp.exp((c - 1.0 - idx) * lg)
    state_ref[...] = jnp.zeros(state_ref.shape, F32)

    def body(n, carry):
        sl = slice(n * c, (n + 1) * c)
        q, k, v = ([ref[0, sl, h * LANES:(h + 1) * LANES] for h in heads]
                   for ref in (q_ref, k_ref, v_ref))
        state = [state_ref[h] for h in heads]
        scores = [lax.dot_general(q[h], k[h], (((1,), (1,)), ((), ())),
                                  preferred_element_type=F32) for h in heads]
        cross = [jnp.dot((q[h].astype(F32) * qdec_ref[h]).astype(BF16),
                         state[h].astype(BF16), preferred_element_type=F32)
                 for h in heads]
        kd_t = [(k[h].astype(F32) * kdec_ref[h]).T.astype(BF16) for h in heads]
        kv = [jnp.dot(kd_t[h], v[h], preferred_element_type=F32) for h in heads]
        inner = [jnp.dot((scores[h] * decay_ref[h]).astype(BF16), v[h],
                         preferred_element_type=F32) for h in heads]
        for h in heads:
            chunk_decay = jnp.exp(jnp.full((1, LANES), float(c), F32) * lg_ref[h])
            state_ref[h] = chunk_decay * state[h] + kv[h]
        for h in heads:
            o = inner[h] + cross[h]
            mu = jnp.mean(o, axis=-1, keepdims=True)
            ctr = o - mu
            var = jnp.mean(ctr * ctr, axis=-1, keepdims=True)
            gate = g_ref[0, sl, h * LANES:(h + 1) * LANES]
            o_ref[0, sl, h * LANES:(h + 1) * LANES] = (
                ctr * lax.rsqrt(var + GN_EPS)
                * (gate * jax.nn.sigmoid(gate))).astype(o_ref.dtype)
        return carry

    for n in range(seq // c):
        body(n, 0)


def _retention(log_gamma, q, k, v, g):
    b, seq, width = q.shape
    spec = pl.BlockSpec((1, seq, width), lambda bi: (bi, 0, 0))
    c = RET_CHUNK
    return pl.pallas_call(
        _ret_kernel,
        grid=(b,),
        in_specs=[pl.BlockSpec(memory_space=pltpu.SMEM), spec, spec, spec, spec],
        out_specs=spec,
        out_shape=jax.ShapeDtypeStruct((b, seq, RET_WIDTH), BF16),
        scratch_shapes=[pltpu.VMEM((RET_HEADS, c, c), F32),
                        pltpu.VMEM((RET_HEADS, c, LANES), F32),
                        pltpu.VMEM((RET_HEADS, c, LANES), F32),
                        pltpu.VMEM((RET_HEADS, RET_KEY_DIM, LANES), F32)],
        compiler_params=pltpu.CompilerParams(
            dimension_semantics=("arbitrary",), vmem_limit_bytes=VMEM_LIMIT),
        name="retention",
    )(log_gamma, q, k, v, g)


def _merge_kernel(n_gate_blocks, x_ref, osb_ref, or_ref, gpre_ref, *refs):
    gate32_refs, refs = refs[:n_gate_blocks], refs[n_gate_blocks:]
    (wsb32_ref, wret32_ref, wout32_ref, gpost_ref, o_ref,
     wgate_ref, wsb_ref, wret_ref, wout_ref) = refs

    @pl.when(pl.program_id(0) == 0)
    def _():
        width = gate32_refs[0].shape[1]
        for b, ref in enumerate(gate32_refs):
            wgate_ref[:, b * width:(b + 1) * width] = ref[...].astype(BF16)
        wsb_ref[...] = wsb32_ref[...].astype(BF16)
        wret_ref[...] = wret32_ref[...].astype(BF16)
        wout_ref[...] = wout32_ref[...].astype(BF16)

    for r in range(MIX_TM // ROW_GROUP):
        rows = slice(r * ROW_GROUP, (r + 1) * ROW_GROUP)
        x = x_ref[rows, :]
        p_sb = jnp.dot(osb_ref[rows, :], wsb_ref[...], preferred_element_type=F32)
        p_ret = jnp.dot(or_ref[rows, :], wret_ref[...], preferred_element_type=F32)
        h = _rms(x, gpre_ref[...]).astype(BF16)
        gate_sb = jnp.dot(h, wgate_ref[:, :D_MODEL], preferred_element_type=F32)
        merged = jax.nn.sigmoid(gate_sb) * p_sb
        gate_ret = jnp.dot(h, wgate_ref[:, D_MODEL:], preferred_element_type=F32)
        merged = merged + jax.nn.sigmoid(gate_ret) * p_ret
        y = jnp.dot(merged.astype(BF16), wout_ref[...], preferred_element_type=F32)
        o_ref[rows, :] = x + _rms(y, gpost_ref[...])


def _merge(x2d, o_sb, o_r, g_pre, w_in, w_sb, w_ret, w_out, g_post):
    m = x2d.shape[0]
    gate_lo = w_in.shape[1] - 2 * D_MODEL
    gate_block = math.gcd(gate_lo, 2 * D_MODEL)
    n_gate_blocks = 2 * D_MODEL // gate_block

    def tile(width):
        return pl.BlockSpec((MIX_TM, width), lambda i: (i, 0))

    def gate_cols(b):
        return pl.BlockSpec((D_MODEL, gate_block),
                            lambda i: (0, gate_lo // gate_block + b),
                            pipeline_mode=pl.Buffered(1))

    gain = pl.BlockSpec((1, D_MODEL), lambda i: (0, 0))
    return pl.pallas_call(
        functools.partial(_merge_kernel, n_gate_blocks),
        grid=(m // MIX_TM,),
        in_specs=[tile(D_MODEL), tile(SB_WIDTH), tile(RET_WIDTH), gain]
                 + [gate_cols(b) for b in range(n_gate_blocks)]
                 + [_resident((SB_WIDTH, D_MODEL)), _resident((RET_WIDTH, D_MODEL)),
                    _resident((D_MODEL, D_MODEL)), gain],
        out_specs=tile(D_MODEL),
        out_shape=jax.ShapeDtypeStruct((m, D_MODEL), F32),
        scratch_shapes=[pltpu.VMEM((D_MODEL, 2 * D_MODEL), BF16),
                        pltpu.VMEM((SB_WIDTH, D_MODEL), BF16),
                        pltpu.VMEM((RET_WIDTH, D_MODEL), BF16),
                        pltpu.VMEM((D_MODEL, D_MODEL), BF16)],
        compiler_params=pltpu.CompilerParams(
            dimension_semantics=("arbitrary",), vmem_limit_bytes=VMEM_LIMIT),
        name="merge",
    )(x2d, o_sb, o_r, g_pre, *([w_in] * n_gate_blocks), w_sb, w_ret, w_out, g_post)


def _rope_tables(seq):
    half = RET_KEY_DIM // 2
    pos = np.arange(seq, dtype=np.float64)
    inv_freq = ROPE_BASE ** (-np.arange(half, dtype=np.float64) / half)
    ang = pos[:, None] * inv_freq[None, :]
    cos, sin = np.cos(ang), np.sin(ang)
    return (jnp.asarray(np.concatenate([cos, cos], axis=-1), dtype=F32),
            jnp.asarray(np.concatenate([-sin, sin], axis=-1), dtype=F32))


def _suffix_sum_matrix():
    j = np.arange(SB_SPAN)[:, None]
    s = np.arange(SB_SPAN)[None, :]
    return jnp.asarray(-(j > s).astype(np.float32), dtype=BF16)


def kernel(x, g_ffn1_pre, g_ffn1_post, w_ffn1_gate, w_ffn1_up, w_ffn1_down,
           g_mix_pre, w_in, w_proj_sb, w_proj_ret, w_out, g_mix_post,
           g_ffn2_pre, g_ffn2_post, w_ffn2_gate, w_ffn2_up, w_ffn2_down):
    b, seq, d = x.shape
    depth = w_in.shape[0]
    log_gamma = jnp.log(1.0 - 2.0 ** (-5.0 - jnp.arange(RET_HEADS, dtype=F32)))
    cosf, sinf = _rope_tables(seq)
    tri = _suffix_sum_matrix()
    x2d = x.reshape(b * seq, d)
    for l in range(depth):
        x2d = _ffn(x2d, g_ffn1_pre[l][None], g_ffn1_post[l][None],
                   w_ffn1_gate[l].astype(BF16), w_ffn1_up[l].astype(BF16),
                   w_ffn1_down[l].astype(BF16))
        (q_sb, k_sb, v_sb, q_r, k_r, v_r, g_r), (w2_gate, w2_up, w2_down) = _mix_in(
            x2d, g_mix_pre[l][None], w_in[l], cosf, sinf, seq,
            to_bf16=(w_ffn2_gate[l], w_ffn2_up[l], w_ffn2_down[l]))
        shp = (b, seq, SB_WIDTH)
        o_sb = _sb_attention(q_sb.reshape(shp), k_sb.reshape(shp),
                             v_sb.reshape(shp), tri)
        o_r = _retention(log_gamma, q_r.reshape(shp), k_r.reshape(shp),
                         v_r.reshape(shp), g_r.reshape(shp))
        x2d = _merge(x2d, o_sb.reshape(b * seq, SB_WIDTH),
                     o_r.reshape(b * seq, RET_WIDTH),
                     g_mix_pre[l][None], w_in[l], w_proj_sb[l], w_proj_ret[l],
                     w_out[l], g_mix_post[l][None])
        x2d = _ffn(x2d, g_ffn2_pre[l][None], g_ffn2_post[l][None],
                   w2_gate, w2_up, w2_down)
    return x2d.reshape(b, seq, d)
```

```python
import functools
import math

import numpy as np
import jax
import jax.numpy as jnp
from jax import lax
from jax.experimental import pallas as pl
from jax.experimental.pallas import tpu as pltpu

F32 = jnp.float32
BF16 = jnp.bfloat16

D_MODEL = 1024
D_FF = 4 * D_MODEL
SB_HEADS = 8
SB_HEAD_DIM = 64
SB_WIDTH = SB_HEADS * SB_HEAD_DIM
RET_HEADS = 4
RET_KEY_DIM = 128
RET_WIDTH = RET_HEADS * RET_KEY_DIM
RMS_EPS = 1e-6
GN_EPS = 1e-5
ROPE_BASE = 10000.0

LANES = 128
BF16_SUBLANES = 16
VMEM_LIMIT = 56 * 1024 * 1024

FFN_TM = 1024
FFN_TF = 1024
MIX_IN_TM = 512
MIX_TM = 1024
ROW_GROUP = 256
SB_TQ = 256
SB_TK = LANES
SB_SPAN = SB_TQ
SB_NB = SB_SPAN // SB_TK
SB_MAX_UNROLL = 28
SB_Q_SCALE = SB_HEAD_DIM ** -0.5 * math.log2(math.e)
RET_CHUNK = ROW_GROUP


def _rms(x, g):
    ms = jnp.mean(x * x, axis=-1, keepdims=True)
    return x * lax.rsqrt(ms + RMS_EPS) * g


def _resident(shape):
    nd = len(shape)
    return pl.BlockSpec(shape, lambda *_: (0,) * nd, pipeline_mode=pl.Buffered(1))


def _cast_slabs(srcs, dsts):
    for src, dst in zip(srcs, dsts):
        dst[...] = src[...].astype(dst.dtype)


def _slab_specs(arrays, steps):
    assert all(w.shape[0] % (steps * BF16_SUBLANES) == 0 for w in arrays)
    return [pl.BlockSpec((w.shape[0] // steps, w.shape[1]), lambda i: (i, 0))
            for w in arrays]


def _ffn_kernel(x_ref, gpre_ref, gpost_ref, wg_ref, wu_ref, wd_ref, o_ref):
    groups = x_ref.shape[0] // ROW_GROUP
    rows = [slice(r * ROW_GROUP, (r + 1) * ROW_GROUP) for r in range(groups)]
    x = [x_ref[r, :] for r in rows]
    h = [_rms(xr, gpre_ref[...]).astype(BF16) for xr in x]
    acc = [None] * groups
    for j in range(D_FF // FFN_TF):
        cols = slice(j * FFN_TF, (j + 1) * FFN_TF)
        for r in range(groups):
            g = jnp.dot(h[r], wg_ref[:, cols], preferred_element_type=F32)
            u = jnp.dot(h[r], wu_ref[:, cols], preferred_element_type=F32)
            a = (g * jax.nn.sigmoid(g) * u).astype(BF16)
            d = jnp.dot(a, wd_ref[cols, :], preferred_element_type=F32)
            acc[r] = d if acc[r] is None else acc[r] + d
    for r in range(groups):
        o_ref[rows[r], :] = x[r] + 0.5 * _rms(acc[r], gpost_ref[...])


def _ffn(x2d, g_pre, g_post, wg, wu, wd):
    m = x2d.shape[0]
    tile = pl.BlockSpec((FFN_TM, D_MODEL), lambda i: (i, 0))
    gain = pl.BlockSpec((1, D_MODEL), lambda i: (0, 0))
    return pl.pallas_call(
        _ffn_kernel,
        grid=(m // FFN_TM,),
        in_specs=[tile, gain, gain,
                  _resident((D_MODEL, D_FF)), _resident((D_MODEL, D_FF)),
                  _resident((D_FF, D_MODEL))],
        out_specs=tile,
        out_shape=jax.ShapeDtypeStruct((m, D_MODEL), F32),
        compiler_params=pltpu.CompilerParams(
            dimension_semantics=("arbitrary",), vmem_limit_bytes=VMEM_LIMIT),
        name="ffn",
    )(x2d, g_pre, g_post, wg, wu, wd)

def _mix_in_kernel(n_casts, pos_blocks, lg_ref, x_ref, g_ref, w32_ref, cos_ref, sin_ref,
                   *refs):
    cast_in, refs = refs[:n_casts], refs[n_casts:]
    qsb_ref, ksb_ref, vsb_ref, or_ref = refs[:4]
    cast_out = refs[4:4 + n_casts]
    w_ref, decay_ref, qdec_ref, kdec_ref, state_ref = refs[4 + n_casts:]

    @pl.when(pl.program_id(0) == 0)
    def _():
        w_ref[...] = w32_ref[...].astype(BF16)
        _retention_tables(lg_ref, decay_ref, qdec_ref, kdec_ref)

    @pl.when(pl.program_id(0) % pos_blocks == 0)
    def _():
        state_ref[...] = jnp.zeros(state_ref.shape, F32)

    _cast_slabs(cast_in, cast_out)
    for r in range(MIX_IN_TM // ROW_GROUP):
        rows = slice(r * ROW_GROUP, (r + 1) * ROW_GROUP)
        h = _rms(x_ref[rows, :], g_ref[...]).astype(BF16)

        def mm(lo, width):
            return jnp.dot(h, w_ref[:, lo:lo + width], preferred_element_type=F32)

        def rotary(t):
            cosf = cos_ref[rows, :]
            sinf = sin_ref[rows, :]
            parts = []
            for hd in range(RET_HEADS):
                th = t[:, hd * LANES:(hd + 1) * LANES]
                parts.append(th * cosf + pltpu.roll(th, LANES // 2, axis=1) * sinf)
            return jnp.concatenate(parts, axis=1)

        c = 0
        qsb_ref[rows, :] = (mm(c, SB_WIDTH) * SB_Q_SCALE).astype(BF16); c += SB_WIDTH
        ksb_ref[rows, :] = mm(c, SB_WIDTH).astype(BF16); c += SB_WIDTH
        vsb_ref[rows, :] = mm(c, SB_WIDTH).astype(BF16); c += SB_WIDTH
        q_r = rotary(mm(c, RET_WIDTH)).astype(BF16); c += RET_WIDTH
        k_r = (rotary(mm(c, RET_WIDTH)) * (RET_KEY_DIM ** -0.5)).astype(BF16); c += RET_WIDTH
        v_r = mm(c, RET_WIDTH).astype(BF16); c += RET_WIDTH
        g_r = mm(c, RET_WIDTH)
        or_ref[rows, :] = _retention_chunk(q_r, k_r, v_r, g_r, lg_ref, decay_ref,
                                           qdec_ref, kdec_ref, state_ref)


def _mix_in(x2d, g, w_in, cosf, sinf, log_gamma, seq, to_bf16=()):
    m = x2d.shape[0]
    steps = m // MIX_IN_TM
    pos_blocks = seq // MIX_IN_TM
    c = RET_CHUNK

    def tile(width):
        return pl.BlockSpec((MIX_IN_TM, width), lambda i: (i, 0))

    rope = pl.BlockSpec((MIX_IN_TM, LANES), lambda i: (i % pos_blocks, 0))
    widths = [SB_WIDTH] * 3 + [RET_WIDTH]
    n_cols = 3 * SB_WIDTH + 4 * RET_WIDTH
    slabs = _slab_specs(to_bf16, steps)
    outs = pl.pallas_call(
        functools.partial(_mix_in_kernel, len(to_bf16), pos_blocks),
        grid=(steps,),
        in_specs=[pl.BlockSpec(memory_space=pltpu.SMEM), tile(D_MODEL),
                  pl.BlockSpec((1, D_MODEL), lambda i: (0, 0)),
                  _resident((D_MODEL, n_cols)), rope, rope] + slabs,
        out_specs=[tile(w) for w in widths] + slabs,
        out_shape=[jax.ShapeDtypeStruct((m, w), BF16) for w in widths]
                  + [jax.ShapeDtypeStruct(w.shape, BF16) for w in to_bf16],
        scratch_shapes=[pltpu.VMEM((D_MODEL, n_cols), BF16),
                        pltpu.VMEM((RET_HEADS, c, c), F32),
                        pltpu.VMEM((RET_HEADS, c, LANES), F32),
                        pltpu.VMEM((RET_HEADS, c, LANES), F32),
                        pltpu.VMEM((RET_HEADS, RET_KEY_DIM, LANES), F32)],
        compiler_params=pltpu.CompilerParams(
            dimension_semantics=("arbitrary",), vmem_limit_bytes=VMEM_LIMIT),
        name="mix_in",
    )(log_gamma, x2d, g, w_in, cosf, sinf, *to_bf16)
    return outs[:len(widths)], outs[len(widths):]


def _sb_kernel(dq_ref, dk_ref, eq_ref, ek_ref, q_ref, k_ref, v_ref, tri_ref, o_ref,
               qh_ref, kh_ref, vh_ref, t_ref, lsig_ref, keep_ref, tot_ref, w_ref,
               later_ref, acc_ref):
    seq = q_ref.shape[1]
    pairs = q_ref.shape[2] // LANES
    seq_tiles = seq // SB_TQ
    n_tiles = pairs * seq_tiles
    lane = lax.broadcasted_iota(jnp.int32, (SB_TQ, LANES), 1)
    head0 = lane < SB_HEAD_DIM
    row = lax.broadcasted_iota(jnp.int32, (SB_TQ, SB_TK), 0)
    col = lax.broadcasted_iota(jnp.int32, (SB_TQ, SB_TK), 1)
    rel = [col - row + c * SB_TK for c in range(SB_NB)]
    tri = tri_ref[...]

    def tile_rows(tile):
        return pl.ds(pl.multiple_of(tile * SB_TQ, SB_TQ), SB_TQ)

    for tile in range(n_tiles):
        src = (0, slice((tile % seq_tiles) * SB_TQ, (tile % seq_tiles + 1) * SB_TQ),
               slice((tile // seq_tiles) * LANES, (tile // seq_tiles + 1) * LANES))
        dst = slice(tile * SB_TQ, (tile + 1) * SB_TQ)
        q = q_ref[src]
        v = v_ref[src]
        zero = jnp.zeros_like(q)
        qh_ref[0, dst, :] = jnp.where(head0, q, zero)
        qh_ref[1, dst, :] = jnp.where(head0, zero, q)
        kh_ref[dst, :] = k_ref[src]
        vh_ref[tile, :SB_SPAN, :] = jnp.where(head0, v, zero)
        vh_ref[tile, SB_SPAN:, :] = jnp.where(head0, zero, v)
    w_ref[1] = jnp.zeros(w_ref.shape[1:], w_ref.dtype)

    def scores(hd, qt, kt):
        return lax.dot_general(qh_ref[hd, tile_rows(qt), :], kh_ref[tile_rows(kt), :],
                               (((1,), (1,)), ((), ())), preferred_element_type=F32)

    def suffix_sums(t, hd, slot, diagonal):
        sp = jnp.maximum(t, 0.0) + jnp.log2(1.0 + jnp.exp2(-jnp.abs(t)))
        lsig_ref[slot, hd] = t - sp
        if diagonal:
            sp = jnp.concatenate(
                [jnp.where(rel[c] < 0, sp[:, c * SB_TK:(c + 1) * SB_TK], 0.0)
                 for c in range(SB_NB)], axis=1)
        return sp[:, 0:1], jnp.dot(sp.astype(BF16), tri, preferred_element_type=F32)

    def stage_sums(parts, hd, slot):
        sp_first, sums = parts
        keep_ref[slot, hd] = sums
        tot_ref[slot, hd] = jnp.broadcast_to(sums[:, 0:1] - sp_first, (SB_TQ, SB_TK))

    def weights(hd, slot, qt, diagonal):
        run = None if diagonal else later_ref[qt, hd]
        for c in range(SB_NB):
            cols = slice(c * SB_TK, (c + 1) * SB_TK)
            arg = lsig_ref[slot, hd, :, cols] + keep_ref[slot, hd, :, cols]
            w = jnp.exp2(arg if run is None else arg + run)
            if diagonal:
                w = jnp.where(rel[c] < 0, w, 0.0)
            lo = hd * SB_SPAN + c * SB_TK
            w_ref[slot, :, lo:lo + SB_TK] = w.astype(BF16)
        tot = tot_ref[slot, hd]
        later_ref[qt, hd] = tot if run is None else run + tot

    def sweep(sq_ref, sk_ref, n_steps, diagonal):
        def step(s):
            return sq_ref[s + 1], sk_ref[s + 1]

        def weighted_values(s, slot):
            qt, kt = step(s)
            pv = jnp.dot(w_ref[slot], vh_ref[kt], preferred_element_type=F32)
            return qt, pv

        def accumulate(qt, pv):
            if diagonal:
                acc_ref[qt] = pv
            else:
                acc_ref[qt] += pv

        def half(s, slot):
            t = [t_ref[1 - slot, hd] for hd in range(2)]
            t_new = [scores(0, *step(s + 2))]
            sums = [suffix_sums(t[0], 0, 1 - slot, diagonal)]
            qt_prev, pv = weighted_values(s - 1, 1 - slot)
            t_new.append(scores(1, *step(s + 2)))
            sums.append(suffix_sums(t[1], 1, 1 - slot, diagonal))
            accumulate(qt_prev, pv)
            qt, _ = step(s)
            for hd in range(2):
                weights(hd, slot, qt, diagonal)
            for hd in range(2):
                t_ref[slot, hd] = t_new[hd]
            for hd in range(2):
                stage_sums(sums[hd], hd, 1 - slot)

        for hd in range(2):
            stage_sums(suffix_sums(scores(hd, *step(0)), hd, 0, diagonal), hd, 0)
        for hd in range(2):
            t_ref[1, hd] = scores(hd, *step(1))

        unroll = _sb_unroll(n_steps)

        def body(i, carry):
            for u in range(unroll):
                half(unroll * i + u, u % 2)
            return carry

        lax.fori_loop(0, n_steps // unroll, body, 0)
        accumulate(*weighted_values(n_steps - 1, 1))

    sweep(dq_ref, dk_ref, n_tiles, True)
    sweep(eq_ref, ek_ref, pairs * seq_tiles * (seq_tiles - 1) // 2, False)
    for tile in range(n_tiles):
        o_ref[0, (tile % seq_tiles) * SB_TQ:(tile % seq_tiles + 1) * SB_TQ,
              (tile // seq_tiles) * LANES:(tile // seq_tiles + 1) * LANES] = (
                  acc_ref[tile].astype(o_ref.dtype))


def _sb_unroll(n_steps):
    return max(u for u in range(2, SB_MAX_UNROLL + 1, 2) if n_steps % u == 0)


def _sb_steps(pairs, seq_tiles):
    n_tiles = pairs * seq_tiles
    diag = [(g, g) for g in range(n_tiles)]
    early = [(p * seq_tiles + t, p * seq_tiles + kt) for p in range(pairs)
             for t in range(1, seq_tiles) for kt in range(t - 1, -1, -1)]
    tables = []
    for steps in (diag, early):
        steps = [(n_tiles, 0)] + steps + [steps[-1]] * 2
        tables += [jnp.asarray([s[0] for s in steps], jnp.int32),
                   jnp.asarray([s[1] for s in steps], jnp.int32)]
    return tables


def _sb_attention(q, k, v, tri):
    b, seq, width = q.shape
    pairs = width // LANES
    seq_tiles = seq // SB_TQ
    n_tiles = pairs * seq_tiles
    spec = pl.BlockSpec((1, seq, width), lambda bi: (bi, 0, 0))
    smem = pl.BlockSpec(memory_space=pltpu.SMEM)
    return pl.pallas_call(
        _sb_kernel,
        grid=(b,),
        in_specs=[smem, smem, smem, smem, spec, spec, spec,
                  pl.BlockSpec((SB_SPAN, SB_SPAN), lambda bi: (0, 0))],
        out_specs=spec,
        out_shape=jax.ShapeDtypeStruct((b, seq, width), BF16),
        scratch_shapes=[pltpu.VMEM((2, n_tiles * SB_TQ, LANES), BF16),
                        pltpu.VMEM((n_tiles * SB_TQ, LANES), BF16),
                        pltpu.VMEM((n_tiles, 2 * SB_SPAN, LANES), BF16),
                        pltpu.VMEM((2, 2, SB_TQ, SB_SPAN), F32),
                        pltpu.VMEM((2, 2, SB_TQ, SB_SPAN), F32),
                        pltpu.VMEM((2, 2, SB_TQ, SB_SPAN), F32),
                        pltpu.VMEM((2, 2, SB_TQ, SB_TK), F32),
                        pltpu.VMEM((2, SB_TQ, 2 * SB_SPAN), BF16),
                        pltpu.VMEM((n_tiles, 2, SB_TQ, SB_TK), F32),
                        pltpu.VMEM((n_tiles + 1, SB_TQ, LANES), F32)],
        compiler_params=pltpu.CompilerParams(
            dimension_semantics=("arbitrary",), vmem_limit_bytes=VMEM_LIMIT),
        name="sb_attn",
    )(*_sb_steps(pairs, seq_tiles), q, k, v, tri)


def _retention_tables(lg_ref, decay_ref, qdec_ref, kdec_ref):
    c = RET_CHUNK
    diff = (lax.broadcasted_iota(jnp.int32, (c, c), 0)
            - lax.broadcasted_iota(jnp.int32, (c, c), 1)).astype(F32)
    idx = lax.broadcasted_iota(jnp.int32, (c, LANES), 0).astype(F32)
    for h in range(RET_HEADS):
        lg = lg_ref[h]
        decay_ref[h] = jnp.where(diff >= 0, jnp.exp(jnp.maximum(diff, 0.0) * lg), 0.0)
        qdec_ref[h] = jnp.exp((idx + 1.0) * lg)
        kdec_ref[h] = jnp.exp((c - 1.0 - idx) * lg)


def _retention_chunk(q, k, v, gate, lg_ref, decay_ref, qdec_ref, kdec_ref, state_ref):
    c = RET_CHUNK
    heads = range(RET_HEADS)
    q, k, v = ([a[:, h * LANES:(h + 1) * LANES] for h in heads] for a in (q, k, v))
    state = [state_ref[h] for h in heads]
    scores = [lax.dot_general(q[h], k[h], (((1,), (1,)), ((), ())),
                              preferred_element_type=F32) for h in heads]
    cross = [jnp.dot((q[h].astype(F32) * qdec_ref[h]).astype(BF16),
                     state[h].astype(BF16), preferred_element_type=F32)
             for h in heads]
    kd_t = [(k[h].astype(F32) * kdec_ref[h]).T.astype(BF16) for h in heads]
    kv = [jnp.dot(kd_t[h], v[h], preferred_element_type=F32) for h in heads]
    inner = [jnp.dot((scores[h] * decay_ref[h]).astype(BF16), v[h],
                     preferred_element_type=F32) for h in heads]
    for h in heads:
        chunk_decay = jnp.exp(jnp.full((1, LANES), float(c), F32) * lg_ref[h])
        state_ref[h] = chunk_decay * state[h] + kv[h]
    out = []
    for h in heads:
        o = inner[h] + cross[h]
        mu = jnp.mean(o, axis=-1, keepdims=True)
        ctr = o - mu
        var = jnp.mean(ctr * ctr, axis=-1, keepdims=True)
        g = gate[:, h * LANES:(h + 1) * LANES]
        out.append((ctr * lax.rsqrt(var + GN_EPS) * (g * jax.nn.sigmoid(g))).astype(BF16))
    return jnp.concatenate(out, axis=1)


def _merge_kernel(n_gate_blocks, x_ref, osb_ref, or_ref, gpre_ref, *refs):
    gate32_refs, refs = refs[:n_gate_blocks], refs[n_gate_blocks:]
    (wsb32_ref, wret32_ref, wout32_ref, gpost_ref, o_ref,
     wgate_ref, wsb_ref, wret_ref, wout_ref) = refs

    @pl.when(pl.program_id(0) == 0)
    def _():
        width = gate32_refs[0].shape[1]
        for b, ref in enumerate(gate32_refs):
            wgate_ref[:, b * width:(b + 1) * width] = ref[...].astype(BF16)
        wsb_ref[...] = wsb32_ref[...].astype(BF16)
        wret_ref[...] = wret32_ref[...].astype(BF16)
        wout_ref[...] = wout32_ref[...].astype(BF16)

    for r in range(MIX_TM // ROW_GROUP):
        rows = slice(r * ROW_GROUP, (r + 1) * ROW_GROUP)
        x = x_ref[rows, :]
        p_sb = jnp.dot(osb_ref[rows, :], wsb_ref[...], preferred_element_type=F32)
        p_ret = jnp.dot(or_ref[rows, :], wret_ref[...], preferred_element_type=F32)
        h = _rms(x, gpre_ref[...]).astype(BF16)
        gate_sb = jnp.dot(h, wgate_ref[:, :D_MODEL], preferred_element_type=F32)
        merged = jax.nn.sigmoid(gate_sb) * p_sb
        gate_ret = jnp.dot(h, wgate_ref[:, D_MODEL:], preferred_element_type=F32)
        merged = merged + jax.nn.sigmoid(gate_ret) * p_ret
        y = jnp.dot(merged.astype(BF16), wout_ref[...], preferred_element_type=F32)
        o_ref[rows, :] = x + _rms(y, gpost_ref[...])


def _merge(x2d, o_sb, o_r, g_pre, w_in, w_sb, w_ret, w_out, g_post):
    m = x2d.shape[0]
    gate_lo = w_in.shape[1] - 2 * D_MODEL
    gate_block = math.gcd(gate_lo, 2 * D_MODEL)
    n_gate_blocks = 2 * D_MODEL // gate_block

    def tile(width):
        return pl.BlockSpec((MIX_TM, width), lambda i: (i, 0))

    def gate_cols(b):
        return pl.BlockSpec((D_MODEL, gate_block),
                            lambda i: (0, gate_lo // gate_block + b),
                            pipeline_mode=pl.Buffered(1))

    gain = pl.BlockSpec((1, D_MODEL), lambda i: (0, 0))
    return pl.pallas_call(
        functools.partial(_merge_kernel, n_gate_blocks),
        grid=(m // MIX_TM,),
        in_specs=[tile(D_MODEL), tile(SB_WIDTH), tile(RET_WIDTH), gain]
                 + [gate_cols(b) for b in range(n_gate_blocks)]
                 + [_resident((SB_WIDTH, D_MODEL)), _resident((RET_WIDTH, D_MODEL)),
                    _resident((D_MODEL, D_MODEL)), gain],
        out_specs=tile(D_MODEL),
        out_shape=jax.ShapeDtypeStruct((m, D_MODEL), F32),
        scratch_shapes=[pltpu.VMEM((D_MODEL, 2 * D_MODEL), BF16),
                        pltpu.VMEM((SB_WIDTH, D_MODEL), BF16),
                        pltpu.VMEM((RET_WIDTH, D_MODEL), BF16),
                        pltpu.VMEM((D_MODEL, D_MODEL), BF16)],
        compiler_params=pltpu.CompilerParams(
            dimension_semantics=("arbitrary",), vmem_limit_bytes=VMEM_LIMIT),
        name="merge",
    )(x2d, o_sb, o_r, g_pre, *([w_in] * n_gate_blocks), w_sb, w_ret, w_out, g_post)


def _rope_tables(seq):
    half = RET_KEY_DIM // 2
    pos = np.arange(seq, dtype=np.float64)
    inv_freq = ROPE_BASE ** (-np.arange(half, dtype=np.float64) / half)
    ang = pos[:, None] * inv_freq[None, :]
    cos, sin = np.cos(ang), np.sin(ang)
    return (jnp.asarray(np.concatenate([cos, cos], axis=-1), dtype=F32),
            jnp.asarray(np.concatenate([-sin, sin], axis=-1), dtype=F32))


def _suffix_sum_matrix():
    j = np.arange(SB_SPAN)[:, None]
    s = np.arange(SB_SPAN)[None, :]
    return jnp.asarray(-(j > s).astype(np.float32), dtype=BF16)


def kernel(x, g_ffn1_pre, g_ffn1_post, w_ffn1_gate, w_ffn1_up, w_ffn1_down,
           g_mix_pre, w_in, w_proj_sb, w_proj_ret, w_out, g_mix_post,
           g_ffn2_pre, g_ffn2_post, w_ffn2_gate, w_ffn2_up, w_ffn2_down):
    b, seq, d = x.shape
    depth = w_in.shape[0]
    log_gamma = jnp.log(1.0 - 2.0 ** (-5.0 - jnp.arange(RET_HEADS, dtype=F32)))
    cosf, sinf = _rope_tables(seq)
    tri = _suffix_sum_matrix()
    x2d = x.reshape(b * seq, d)
    for l in range(depth):
        x2d = _ffn(x2d, g_ffn1_pre[l][None], g_ffn1_post[l][None],
                   w_ffn1_gate[l].astype(BF16), w_ffn1_up[l].astype(BF16),
                   w_ffn1_down[l].astype(BF16))
        (q_sb, k_sb, v_sb, o_r), (w2_gate, w2_up, w2_down) = _mix_in(
            x2d, g_mix_pre[l][None], w_in[l], cosf, sinf, log_gamma, seq,
            to_bf16=(w_ffn2_gate[l], w_ffn2_up[l], w_ffn2_down[l]))
        shp = (b, seq, SB_WIDTH)
        o_sb = _sb_attention(q_sb.reshape(shp), k_sb.reshape(shp),
                             v_sb.reshape(shp), tri)
        x2d = _merge(x2d, o_sb.reshape(b * seq, SB_WIDTH), o_r,
                     g_mix_pre[l][None], w_in[l], w_proj_sb[l], w_proj_ret[l],
                     w_out[l], g_mix_post[l][None])
        x2d = _ffn(x2d, g_ffn2_pre[l][None], g_ffn2_post[l][None],
                   w2_gate, w2_up, w2_down)
    return x2d.reshape(b, seq, d)
```

```python
import functools
import math

import numpy as np
import jax
import jax.numpy as jnp
from jax import lax
from jax.experimental import pallas as pl
from jax.experimental.pallas import tpu as pltpu

F32 = jnp.float32
BF16 = jnp.bfloat16

D_MODEL = 1024
D_FF = 4 * D_MODEL
SB_HEADS = 8
SB_HEAD_DIM = 64
SB_WIDTH = SB_HEADS * SB_HEAD_DIM
RET_HEADS = 4
RET_KEY_DIM = 128
RET_WIDTH = RET_HEADS * RET_KEY_DIM
RMS_EPS = 1e-6
GN_EPS = 1e-5
ROPE_BASE = 10000.0

LANES = 128
BF16_SUBLANES = 16
VMEM_LIMIT = 56 * 1024 * 1024

FFN_TM = 1024
FFN_TF = 1024
MIX_IN_TM = 1024
MIX_TM = 1024
ROW_GROUP = 256
SB_TQ = 256
SB_TK = LANES
SB_SPAN = SB_TQ
SB_NB = SB_SPAN // SB_TK
SB_MAX_UNROLL = 28
SB_Q_SCALE = SB_HEAD_DIM ** -0.5 * math.log2(math.e)
RET_CHUNK = ROW_GROUP


def _rms(x, g):
    ms = jnp.mean(x * x, axis=-1, keepdims=True)
    return x * lax.rsqrt(ms + RMS_EPS) * g


def _resident(shape):
    nd = len(shape)
    return pl.BlockSpec(shape, lambda *_: (0,) * nd, pipeline_mode=pl.Buffered(1))


def _cast_slabs(srcs, dsts):
    for src, dst in zip(srcs, dsts):
        dst[...] = src[...].astype(dst.dtype)


def _slab_specs(arrays, steps):
    assert all(w.shape[0] % (steps * BF16_SUBLANES) == 0 for w in arrays)
    return [pl.BlockSpec((w.shape[0] // steps, w.shape[1]), lambda i: (i, 0))
            for w in arrays]


def _ffn_kernel(x_ref, gpre_ref, gpost_ref, wg_ref, wu_ref, wd_ref, o_ref):
    groups = x_ref.shape[0] // ROW_GROUP
    rows = [slice(r * ROW_GROUP, (r + 1) * ROW_GROUP) for r in range(groups)]
    x = [x_ref[r, :] for r in rows]
    h = [_rms(xr, gpre_ref[...]).astype(BF16) for xr in x]
    acc = [None] * groups
    for j in range(D_FF // FFN_TF):
        cols = slice(j * FFN_TF, (j + 1) * FFN_TF)
        for r in range(groups):
            g = jnp.dot(h[r], wg_ref[:, cols], preferred_element_type=F32)
            u = jnp.dot(h[r], wu_ref[:, cols], preferred_element_type=F32)
            a = (g * jax.nn.sigmoid(g) * u).astype(BF16)
            d = jnp.dot(a, wd_ref[cols, :], preferred_element_type=F32)
            acc[r] = d if acc[r] is None else acc[r] + d
    for r in range(groups):
        o_ref[rows[r], :] = x[r] + 0.5 * _rms(acc[r], gpost_ref[...])


def _ffn(x2d, g_pre, g_post, wg, wu, wd):
    m = x2d.shape[0]
    tile = pl.BlockSpec((FFN_TM, D_MODEL), lambda i: (i, 0))
    gain = pl.BlockSpec((1, D_MODEL), lambda i: (0, 0))
    return pl.pallas_call(
        _ffn_kernel,
        grid=(m // FFN_TM,),
        in_specs=[tile, gain, gain,
                  _resident((D_MODEL, D_FF)), _resident((D_MODEL, D_FF)),
                  _resident((D_FF, D_MODEL))],
        out_specs=tile,
        out_shape=jax.ShapeDtypeStruct((m, D_MODEL), F32),
        compiler_params=pltpu.CompilerParams(
            dimension_semantics=("arbitrary",), vmem_limit_bytes=VMEM_LIMIT),
        name="ffn",
    )(x2d, g_pre, g_post, wg, wu, wd)

def _mix_in_kernel(n_casts, pos_blocks, lg_ref, x_ref, g_ref, w32_ref, cos_ref, sin_ref,
                   *refs):
    cast_in, refs = refs[:n_casts], refs[n_casts:]
    qsb_ref, ksb_ref, vsb_ref, or_ref = refs[:4]
    cast_out = refs[4:4 + n_casts]
    w_ref, decay_ref, qdec_ref, kdec_ref, state_ref = refs[4 + n_casts:]

    @pl.when(pl.program_id(0) == 0)
    def _():
        w_ref[...] = w32_ref[...].astype(BF16)
        _retention_tables(lg_ref, decay_ref, qdec_ref, kdec_ref)

    @pl.when(pl.program_id(0) % pos_blocks == 0)
    def _():
        state_ref[...] = jnp.zeros(state_ref.shape, F32)

    _cast_slabs(cast_in, cast_out)
    for r in range(MIX_IN_TM // ROW_GROUP):
        rows = slice(r * ROW_GROUP, (r + 1) * ROW_GROUP)
        h = _rms(x_ref[rows, :], g_ref[...]).astype(BF16)

        def mm(lo, width):
            return jnp.dot(h, w_ref[:, lo:lo + width], preferred_element_type=F32)

        def rotary(t):
            cosf = cos_ref[rows, :]
            sinf = sin_ref[rows, :]
            parts = []
            for hd in range(RET_HEADS):
                th = t[:, hd * LANES:(hd + 1) * LANES]
                parts.append(th * cosf + pltpu.roll(th, LANES // 2, axis=1) * sinf)
            return jnp.concatenate(parts, axis=1)

        c = 3 * SB_WIDTH
        q_r = rotary(mm(c, RET_WIDTH)).astype(BF16); c += RET_WIDTH
        k_r = (rotary(mm(c, RET_WIDTH)) * (RET_KEY_DIM ** -0.5)).astype(BF16); c += RET_WIDTH
        v_r = mm(c, RET_WIDTH).astype(BF16); c += RET_WIDTH
        g_r = mm(c, RET_WIDTH)
        or_ref[rows, :] = _retention_chunk(q_r, k_r, v_r, g_r, lg_ref, decay_ref,
                                           qdec_ref, kdec_ref, state_ref)
        c = 0
        qsb_ref[rows, :] = (mm(c, SB_WIDTH) * SB_Q_SCALE).astype(BF16); c += SB_WIDTH
        ksb_ref[rows, :] = mm(c, SB_WIDTH).astype(BF16); c += SB_WIDTH
        vsb_ref[rows, :] = mm(c, SB_WIDTH).astype(BF16)


def _mix_in(x2d, g, w_in, cosf, sinf, log_gamma, seq, to_bf16=()):
    m = x2d.shape[0]
    steps = m // MIX_IN_TM
    pos_blocks = seq // MIX_IN_TM
    c = RET_CHUNK

    def tile(width):
        return pl.BlockSpec((MIX_IN_TM, width), lambda i: (i, 0))

    rope = pl.BlockSpec((MIX_IN_TM, LANES), lambda i: (i % pos_blocks, 0))
    widths = [SB_WIDTH] * 3 + [RET_WIDTH]
    n_cols = 3 * SB_WIDTH + 4 * RET_WIDTH
    slabs = _slab_specs(to_bf16, steps)
    outs = pl.pallas_call(
        functools.partial(_mix_in_kernel, len(to_bf16), pos_blocks),
        grid=(steps,),
        in_specs=[pl.BlockSpec(memory_space=pltpu.SMEM), tile(D_MODEL),
                  pl.BlockSpec((1, D_MODEL), lambda i: (0, 0)),
                  _resident((D_MODEL, n_cols)), rope, rope] + slabs,
        out_specs=[tile(w) for w in widths] + slabs,
        out_shape=[jax.ShapeDtypeStruct((m, w), BF16) for w in widths]
                  + [jax.ShapeDtypeStruct(w.shape, BF16) for w in to_bf16],
        scratch_shapes=[pltpu.VMEM((D_MODEL, n_cols), BF16),
                        pltpu.VMEM((RET_HEADS, c, c), F32),
                        pltpu.VMEM((RET_HEADS, c, LANES), F32),
                        pltpu.VMEM((RET_HEADS, c, LANES), F32),
                        pltpu.VMEM((RET_HEADS, RET_KEY_DIM, LANES), F32)],
        compiler_params=pltpu.CompilerParams(
            dimension_semantics=("arbitrary",), vmem_limit_bytes=VMEM_LIMIT),
        name="mix_in",
    )(log_gamma, x2d, g, w_in, cosf, sinf, *to_bf16)
    return outs[:len(widths)], outs[len(widths):]


def _sb_kernel(dq_ref, dk_ref, eq_ref, ek_ref, q_ref, k_ref, v_ref, tri_ref, o_ref,
               qh_ref, kh_ref, vh_ref, t_ref, lsig_ref, keep_ref, tot_ref, w_ref,
               later_ref, acc_ref):
    seq = q_ref.shape[1]
    pairs = q_ref.shape[2] // LANES
    seq_tiles = seq // SB_TQ
    n_tiles = pairs * seq_tiles
    lane = lax.broadcasted_iota(jnp.int32, (SB_TQ, LANES), 1)
    head0 = lane < SB_HEAD_DIM
    row = lax.broadcasted_iota(jnp.int32, (SB_TQ, SB_TK), 0)
    col = lax.broadcasted_iota(jnp.int32, (SB_TQ, SB_TK), 1)
    rel = [col - row + c * SB_TK for c in range(SB_NB)]
    tri = tri_ref[...]

    def tile_rows(tile):
        return pl.ds(pl.multiple_of(tile * SB_TQ, SB_TQ), SB_TQ)

    for tile in range(n_tiles):
        src = (0, slice((tile % seq_tiles) * SB_TQ, (tile % seq_tiles + 1) * SB_TQ),
               slice((tile // seq_tiles) * LANES, (tile // seq_tiles + 1) * LANES))
        dst = slice(tile * SB_TQ, (tile + 1) * SB_TQ)
        q = q_ref[src]
        v = v_ref[src]
        zero = jnp.zeros_like(q)
        qh_ref[0, dst, :] = jnp.where(head0, q, zero)
        qh_ref[1, dst, :] = jnp.where(head0, zero, q)
        kh_ref[dst, :] = k_ref[src]
        vh_ref[tile, :SB_SPAN, :] = jnp.where(head0, v, zero)
        vh_ref[tile, SB_SPAN:, :] = jnp.where(head0, zero, v)
    w_ref[1] = jnp.zeros(w_ref.shape[1:], w_ref.dtype)

    def scores(hd, qt, kt):
        return lax.dot_general(qh_ref[hd, tile_rows(qt), :], kh_ref[tile_rows(kt), :],
                               (((1,), (1,)), ((), ())), preferred_element_type=F32)

    def suffix_sums(t, hd, slot, diagonal):
        sp = jnp.maximum(t, 0.0) + jnp.log2(1.0 + jnp.exp2(-jnp.abs(t)))
        lsig_ref[slot, hd] = t - sp
        if diagonal:
            sp = jnp.concatenate(
                [jnp.where(rel[c] < 0, sp[:, c * SB_TK:(c + 1) * SB_TK], 0.0)
                 for c in range(SB_NB)], axis=1)
        return sp[:, 0:1], jnp.dot(sp.astype(BF16), tri, preferred_element_type=F32)

    def stage_sums(parts, hd, slot):
        sp_first, sums = parts
        keep_ref[slot, hd] = sums
        tot_ref[slot, hd] = jnp.broadcast_to(sums[:, 0:1] - sp_first, (SB_TQ, SB_TK))

    def weights(hd, slot, qt, diagonal):
        run = None if diagonal else later_ref[qt, hd]
        for c in range(SB_NB):
            cols = slice(c * SB_TK, (c + 1) * SB_TK)
            arg = lsig_ref[slot, hd, :, cols] + keep_ref[slot, hd, :, cols]
            w = jnp.exp2(arg if run is None else arg + run)
            if diagonal:
                w = jnp.where(rel[c] < 0, w, 0.0)
            lo = hd * SB_SPAN + c * SB_TK
            w_ref[slot, :, lo:lo + SB_TK] = w.astype(BF16)
        tot = tot_ref[slot, hd]
        later_ref[qt, hd] = tot if run is None else run + tot

    def sweep(sq_ref, sk_ref, n_steps, diagonal):
        def step(s):
            return sq_ref[s + 1], sk_ref[s + 1]

        def weighted_values(s, slot):
            qt, kt = step(s)
            pv = jnp.dot(w_ref[slot], vh_ref[kt], preferred_element_type=F32)
            return qt, pv

        def accumulate(qt, pv):
            if diagonal:
                acc_ref[qt] = pv
            else:
                acc_ref[qt] += pv

        def half(s, slot):
            t = [t_ref[1 - slot, hd] for hd in range(2)]
            t_new = [scores(0, *step(s + 2))]
            sums = [suffix_sums(t[0], 0, 1 - slot, diagonal)]
            qt_prev, pv = weighted_values(s - 1, 1 - slot)
            t_new.append(scores(1, *step(s + 2)))
            sums.append(suffix_sums(t[1], 1, 1 - slot, diagonal))
            accumulate(qt_prev, pv)
            qt, _ = step(s)
            for hd in range(2):
                weights(hd, slot, qt, diagonal)
            for hd in range(2):
                t_ref[slot, hd] = t_new[hd]
            for hd in range(2):
                stage_sums(sums[hd], hd, 1 - slot)

        for hd in range(2):
            stage_sums(suffix_sums(scores(hd, *step(0)), hd, 0, diagonal), hd, 0)
        for hd in range(2):
            t_ref[1, hd] = scores(hd, *step(1))

        unroll = _sb_unroll(n_steps)

        def body(i, carry):
            for u in range(unroll):
                half(unroll * i + u, u % 2)
            return carry

        lax.fori_loop(0, n_steps // unroll, body, 0)
        accumulate(*weighted_values(n_steps - 1, 1))

    sweep(dq_ref, dk_ref, n_tiles, True)
    sweep(eq_ref, ek_ref, pairs * seq_tiles * (seq_tiles - 1) // 2, False)
    for tile in range(n_tiles):
        o_ref[0, (tile % seq_tiles) * SB_TQ:(tile % seq_tiles + 1) * SB_TQ,
              (tile // seq_tiles) * LANES:(tile // seq_tiles + 1) * LANES] = (
                  acc_ref[tile].astype(o_ref.dtype))


def _sb_unroll(n_steps):
    return max(u for u in range(2, SB_MAX_UNROLL + 1, 2) if n_steps % u == 0)


def _sb_steps(pairs, seq_tiles):
    n_tiles = pairs * seq_tiles
    diag = [(g, g) for g in range(n_tiles)]
    early = [(p * seq_tiles + t, p * seq_tiles + kt) for p in range(pairs)
             for t in range(1, seq_tiles) for kt in range(t - 1, -1, -1)]
    tables = []
    for steps in (diag, early):
        steps = [(n_tiles, 0)] + steps + [steps[-1]] * 2
        tables += [jnp.asarray([s[0] for s in steps], jnp.int32),
                   jnp.asarray([s[1] for s in steps], jnp.int32)]
    return tables


def _sb_attention(q, k, v, tri):
    b, seq, width = q.shape
    pairs = width // LANES
    seq_tiles = seq // SB_TQ
    n_tiles = pairs * seq_tiles
    spec = pl.BlockSpec((1, seq, width), lambda bi: (bi, 0, 0))
    smem = pl.BlockSpec(memory_space=pltpu.SMEM)
    return pl.pallas_call(
        _sb_kernel,
        grid=(b,),
        in_specs=[smem, smem, smem, smem, spec, spec, spec,
                  pl.BlockSpec((SB_SPAN, SB_SPAN), lambda bi: (0, 0))],
        out_specs=spec,
        out_shape=jax.ShapeDtypeStruct((b, seq, width), BF16),
        scratch_shapes=[pltpu.VMEM((2, n_tiles * SB_TQ, LANES), BF16),
                        pltpu.VMEM((n_tiles * SB_TQ, LANES), BF16),
                        pltpu.VMEM((n_tiles, 2 * SB_SPAN, LANES), BF16),
                        pltpu.VMEM((2, 2, SB_TQ, SB_SPAN), F32),
                        pltpu.VMEM((2, 2, SB_TQ, SB_SPAN), F32),
                        pltpu.VMEM((2, 2, SB_TQ, SB_SPAN), F32),
                        pltpu.VMEM((2, 2, SB_TQ, SB_TK), F32),
                        pltpu.VMEM((2, SB_TQ, 2 * SB_SPAN), BF16),
                        pltpu.VMEM((n_tiles, 2, SB_TQ, SB_TK), F32),
                        pltpu.VMEM((n_tiles + 1, SB_TQ, LANES), F32)],
        compiler_params=pltpu.CompilerParams(
            dimension_semantics=("arbitrary",), vmem_limit_bytes=VMEM_LIMIT),
        name="sb_attn",
    )(*_sb_steps(pairs, seq_tiles), q, k, v, tri)


def _retention_tables(lg_ref, decay_ref, qdec_ref, kdec_ref):
    c = RET_CHUNK
    diff = (lax.broadcasted_iota(jnp.int32, (c, c), 0)
            - lax.broadcasted_iota(jnp.int32, (c, c), 1)).astype(F32)
    idx = lax.broadcasted_iota(jnp.int32, (c, LANES), 0).astype(F32)
    for h in range(RET_HEADS):
        lg = lg_ref[h]
        decay_ref[h] = jnp.where(diff >= 0, jnp.exp(jnp.maximum(diff, 0.0) * lg), 0.0)
        qdec_ref[h] = jnp.exp((idx + 1.0) * lg)
        kdec_ref[h] = jnp.exp((c - 1.0 - idx) * lg)


def _retention_chunk(q, k, v, gate, lg_ref, decay_ref, qdec_ref, kdec_ref, state_ref):
    c = RET_CHUNK
    heads = range(RET_HEADS)
    q, k, v = ([a[:, h * LANES:(h + 1) * LANES] for h in heads] for a in (q, k, v))
    state = [state_ref[h] for h in heads]
    scores = [lax.dot_general(q[h], k[h], (((1,), (1,)), ((), ())),
                              preferred_element_type=F32) for h in heads]
    cross = [jnp.dot((q[h].astype(F32) * qdec_ref[h]).astype(BF16),
                     state[h].astype(BF16), preferred_element_type=F32)
             for h in heads]
    kd_t = [(k[h].astype(F32) * kdec_ref[h]).T.astype(BF16) for h in heads]
    kv = [jnp.dot(kd_t[h], v[h], preferred_element_type=F32) for h in heads]
    inner = [jnp.dot((scores[h] * decay_ref[h]).astype(BF16), v[h],
                     preferred_element_type=F32) for h in heads]
    for h in heads:
        chunk_decay = jnp.exp(jnp.full((1, LANES), float(c), F32) * lg_ref[h])
        state_ref[h] = chunk_decay * state[h] + kv[h]
    out = []
    for h in heads:
        o = inner[h] + cross[h]
        mu = jnp.mean(o, axis=-1, keepdims=True)
        ctr = o - mu
        var = jnp.mean(ctr * ctr, axis=-1, keepdims=True)
        g = gate[:, h * LANES:(h + 1) * LANES]
        out.append((ctr * lax.rsqrt(var + GN_EPS) * (g * jax.nn.sigmoid(g))).astype(BF16))
    return jnp.concatenate(out, axis=1)


def _merge_kernel(n_gate_blocks, x_ref, osb_ref, or_ref, gpre_ref, *refs):
    gate32_refs, refs = refs[:n_gate_blocks], refs[n_gate_blocks:]
    (wsb32_ref, wret32_ref, wout32_ref, gpost_ref, o_ref,
     wgate_ref, wsb_ref, wret_ref, wout_ref) = refs

    @pl.when(pl.program_id(0) == 0)
    def _():
        width = gate32_refs[0].shape[1]
        for b, ref in enumerate(gate32_refs):
            wgate_ref[:, b * width:(b + 1) * width] = ref[...].astype(BF16)
        wsb_ref[...] = wsb32_ref[...].astype(BF16)
        wret_ref[...] = wret32_ref[...].astype(BF16)
        wout_ref[...] = wout32_ref[...].astype(BF16)

    for r in range(MIX_TM // ROW_GROUP):
        rows = slice(r * ROW_GROUP, (r + 1) * ROW_GROUP)
        x = x_ref[rows, :]
        p_sb = jnp.dot(osb_ref[rows, :], wsb_ref[...], preferred_element_type=F32)
        p_ret = jnp.dot(or_ref[rows, :], wret_ref[...], preferred_element_type=F32)
        h = _rms(x, gpre_ref[...]).astype(BF16)
        gate_sb = jnp.dot(h, wgate_ref[:, :D_MODEL], preferred_element_type=F32)
        merged = jax.nn.sigmoid(gate_sb) * p_sb
        gate_ret = jnp.dot(h, wgate_ref[:, D_MODEL:], preferred_element_type=F32)
        merged = merged + jax.nn.sigmoid(gate_ret) * p_ret
        y = jnp.dot(merged.astype(BF16), wout_ref[...], preferred_element_type=F32)
        o_ref[rows, :] = x + _rms(y, gpost_ref[...])


def _merge(x2d, o_sb, o_r, g_pre, w_in, w_sb, w_ret, w_out, g_post):
    m = x2d.shape[0]
    gate_lo = w_in.shape[1] - 2 * D_MODEL
    gate_block = math.gcd(gate_lo, 2 * D_MODEL)
    n_gate_blocks = 2 * D_MODEL // gate_block

    def tile(width):
        return pl.BlockSpec((MIX_TM, width), lambda i: (i, 0))

    def gate_cols(b):
        return pl.BlockSpec((D_MODEL, gate_block),
                            lambda i: (0, gate_lo // gate_block + b),
                            pipeline_mode=pl.Buffered(1))

    gain = pl.BlockSpec((1, D_MODEL), lambda i: (0, 0))
    return pl.pallas_call(
        functools.partial(_merge_kernel, n_gate_blocks),
        grid=(m // MIX_TM,),
        in_specs=[tile(D_MODEL), tile(SB_WIDTH), tile(RET_WIDTH), gain]
                 + [gate_cols(b) for b in range(n_gate_blocks)]
                 + [_resident((SB_WIDTH, D_MODEL)), _resident((RET_WIDTH, D_MODEL)),
                    _resident((D_MODEL, D_MODEL)), gain],
        out_specs=tile(D_MODEL),
        out_shape=jax.ShapeDtypeStruct((m, D_MODEL), F32),
        scratch_shapes=[pltpu.VMEM((D_MODEL, 2 * D_MODEL), BF16),
                        pltpu.VMEM((SB_WIDTH, D_MODEL), BF16),
                        pltpu.VMEM((RET_WIDTH, D_MODEL), BF16),
                        pltpu.VMEM((D_MODEL, D_MODEL), BF16)],
        compiler_params=pltpu.CompilerParams(
            dimension_semantics=("arbitrary",), vmem_limit_bytes=VMEM_LIMIT),
        name="merge",
    )(x2d, o_sb, o_r, g_pre, *([w_in] * n_gate_blocks), w_sb, w_ret, w_out, g_post)


def _rope_tables(seq):
    half = RET_KEY_DIM // 2
    pos = np.arange(seq, dtype=np.float64)
    inv_freq = ROPE_BASE ** (-np.arange(half, dtype=np.float64) / half)
    ang = pos[:, None] * inv_freq[None, :]
    cos, sin = np.cos(ang), np.sin(ang)
    return (jnp.asarray(np.concatenate([cos, cos], axis=-1), dtype=F32),
            jnp.asarray(np.concatenate([-sin, sin], axis=-1), dtype=F32))


def _suffix_sum_matrix():
    j = np.arange(SB_SPAN)[:, None]
    s = np.arange(SB_SPAN)[None, :]
    return jnp.asarray(-(j > s).astype(np.float32), dtype=BF16)


def kernel(x, g_ffn1_pre, g_ffn1_post, w_ffn1_gate, w_ffn1_up, w_ffn1_down,
           g_mix_pre, w_in, w_proj_sb, w_proj_ret, w_out, g_mix_post,
           g_ffn2_pre, g_ffn2_post, w_ffn2_gate, w_ffn2_up, w_ffn2_down):
    b, seq, d = x.shape
    depth = w_in.shape[0]
    log_gamma = jnp.log(1.0 - 2.0 ** (-5.0 - jnp.arange(RET_HEADS, dtype=F32)))
    cosf, sinf = _rope_tables(seq)
    tri = _suffix_sum_matrix()
    x2d = x.reshape(b * seq, d)
    for l in range(depth):
        x2d = _ffn(x2d, g_ffn1_pre[l][None], g_ffn1_post[l][None],
                   w_ffn1_gate[l].astype(BF16), w_ffn1_up[l].astype(BF16),
                   w_ffn1_down[l].astype(BF16))
        (q_sb, k_sb, v_sb, o_r), (w2_gate, w2_up, w2_down) = _mix_in(
            x2d, g_mix_pre[l][None], w_in[l], cosf, sinf, log_gamma, seq,
            to_bf16=(w_ffn2_gate[l], w_ffn2_up[l], w_ffn2_down[l]))
        shp = (b, seq, SB_WIDTH)
        o_sb = _sb_attention(q_sb.reshape(shp), k_sb.reshape(shp),
                             v_sb.reshape(shp), tri)
        x2d = _merge(x2d, o_sb.reshape(b * seq, SB_WIDTH), o_r,
                     g_mix_pre[l][None], w_in[l], w_proj_sb[l], w_proj_ret[l],
                     w_out[l], g_mix_post[l][None])
        x2d = _ffn(x2d, g_ffn2_pre[l][None], g_ffn2_post[l][None],
                   w2_gate, w2_up, w2_down)
    return x2d.reshape(b, seq, d)
```

```python
import functools
import math

import numpy as np
import jax
import jax.numpy as jnp
from jax import lax
from jax.experimental import pallas as pl
from jax.experimental.pallas import tpu as pltpu

F32 = jnp.float32
BF16 = jnp.bfloat16

D_MODEL = 1024
D_FF = 4 * D_MODEL
SB_HEADS = 8
SB_HEAD_DIM = 64
SB_WIDTH = SB_HEADS * SB_HEAD_DIM
RET_HEADS = 4
RET_KEY_DIM = 128
RET_WIDTH = RET_HEADS * RET_KEY_DIM
RMS_EPS = 1e-6
GN_EPS = 1e-5
ROPE_BASE = 10000.0

LANES = 128
BF16_SUBLANES = 16
VMEM_LIMIT = 56 * 1024 * 1024
SB_VMEM_LIMIT = 48 * 1024 * 1024
MERGE_VMEM_LIMIT = 50 * 1024 * 1024

FFN_TM = 1024
FFN_TF = 1024
MIX_IN_TM = 1024
MIX_TM = 1024
ROW_GROUP = 256
SB_TQ = 256
SB_TK = LANES
SB_SPAN = SB_TQ
SB_NB = SB_SPAN // SB_TK
SB_MAX_UNROLL = 28
SB_Q_SCALE = SB_HEAD_DIM ** -0.5 * math.log2(math.e)
RET_CHUNK = ROW_GROUP


def _rms(x, g):
    ms = jnp.mean(x * x, axis=-1, keepdims=True)
    return x * lax.rsqrt(ms + RMS_EPS) * g


def _resident(shape):
    nd = len(shape)
    return pl.BlockSpec(shape, lambda *_: (0,) * nd, pipeline_mode=pl.Buffered(1))


def _cast_slabs(srcs, dsts):
    for src, dst in zip(srcs, dsts):
        dst[...] = src[...].astype(dst.dtype)


def _slab_specs(arrays, steps):
    assert all(w.shape[0] % (steps * BF16_SUBLANES) == 0 for w in arrays)
    return [pl.BlockSpec((w.shape[0] // steps, w.shape[1]), lambda i: (i, 0))
            for w in arrays]


def _ffn_kernel(x_ref, gpre_ref, gpost_ref, wg_ref, wu_ref, wd_ref, o_ref):
    groups = x_ref.shape[0] // ROW_GROUP
    rows = [slice(r * ROW_GROUP, (r + 1) * ROW_GROUP) for r in range(groups)]
    x = [x_ref[r, :] for r in rows]
    h = [_rms(xr, gpre_ref[...]).astype(BF16) for xr in x]
    acc = [None] * groups
    for j in range(D_FF // FFN_TF):
        cols = slice(j * FFN_TF, (j + 1) * FFN_TF)
        for r in range(groups):
            g = jnp.dot(h[r], wg_ref[:, cols], preferred_element_type=F32)
            u = jnp.dot(h[r], wu_ref[:, cols], preferred_element_type=F32)
            a = (g * jax.nn.sigmoid(g) * u).astype(BF16)
            d = jnp.dot(a, wd_ref[cols, :], preferred_element_type=F32)
            acc[r] = d if acc[r] is None else acc[r] + d
    for r in range(groups):
        o_ref[rows[r], :] = x[r] + 0.5 * _rms(acc[r], gpost_ref[...])


def _ffn(x2d, g_pre, g_post, wg, wu, wd):
    m = x2d.shape[0]
    tile = pl.BlockSpec((FFN_TM, D_MODEL), lambda i: (i, 0))
    gain = pl.BlockSpec((1, D_MODEL), lambda i: (0, 0))
    return pl.pallas_call(
        _ffn_kernel,
        grid=(m // FFN_TM,),
        in_specs=[tile, gain, gain,
                  _resident((D_MODEL, D_FF)), _resident((D_MODEL, D_FF)),
                  _resident((D_FF, D_MODEL))],
        out_specs=tile,
        out_shape=jax.ShapeDtypeStruct((m, D_MODEL), F32),
        compiler_params=pltpu.CompilerParams(
            dimension_semantics=("arbitrary",), vmem_limit_bytes=VMEM_LIMIT),
        name="ffn",
    )(x2d, g_pre, g_post, wg, wu, wd)

def _mix_in_kernel(n_casts, pos_blocks, lg_ref, x_ref, g_ref, w32_ref, cos_ref, sin_ref,
                   *refs):
    cast_in, refs = refs[:n_casts], refs[n_casts:]
    qsb_ref, ksb_ref, vsb_ref, or_ref = refs[:4]
    cast_out = refs[4:4 + n_casts]
    w_ref, decay_ref, qdec_ref, kdec_ref, state_ref = refs[4 + n_casts:]

    @pl.when(pl.program_id(0) == 0)
    def _():
        w_ref[...] = w32_ref[...].astype(BF16)
        _retention_tables(lg_ref, decay_ref, qdec_ref, kdec_ref)

    @pl.when(pl.program_id(0) % pos_blocks == 0)
    def _():
        state_ref[...] = jnp.zeros(state_ref.shape, F32)

    _cast_slabs(cast_in, cast_out)
    for r in range(MIX_IN_TM // ROW_GROUP):
        rows = slice(r * ROW_GROUP, (r + 1) * ROW_GROUP)
        h = _rms(x_ref[rows, :], g_ref[...]).astype(BF16)

        def mm(lo, width):
            return jnp.dot(h, w_ref[:, lo:lo + width], preferred_element_type=F32)

        def rotary(t):
            cosf = cos_ref[rows, :]
            sinf = sin_ref[rows, :]
            parts = []
            for hd in range(RET_HEADS):
                th = t[:, hd * LANES:(hd + 1) * LANES]
                parts.append(th * cosf + pltpu.roll(th, LANES // 2, axis=1) * sinf)
            return jnp.concatenate(parts, axis=1)

        c = 3 * SB_WIDTH
        q_r = rotary(mm(c, RET_WIDTH)).astype(BF16); c += RET_WIDTH
        k_r = (rotary(mm(c, RET_WIDTH)) * (RET_KEY_DIM ** -0.5)).astype(BF16); c += RET_WIDTH
        v_r = mm(c, RET_WIDTH).astype(BF16); c += RET_WIDTH
        g_r = mm(c, RET_WIDTH)
        or_ref[rows, :] = _retention_chunk(q_r, k_r, v_r, g_r, lg_ref, decay_ref,
                                           qdec_ref, kdec_ref, state_ref)
        c = 0
        qsb_ref[rows, :] = (mm(c, SB_WIDTH) * SB_Q_SCALE).astype(BF16); c += SB_WIDTH
        ksb_ref[rows, :] = mm(c, SB_WIDTH).astype(BF16); c += SB_WIDTH
        vsb_ref[rows, :] = mm(c, SB_WIDTH).astype(BF16)


def _mix_in(x2d, g, w_in, cosf, sinf, log_gamma, seq, to_bf16=()):
    m = x2d.shape[0]
    steps = m // MIX_IN_TM
    pos_blocks = seq // MIX_IN_TM
    c = RET_CHUNK

    def tile(width):
        return pl.BlockSpec((MIX_IN_TM, width), lambda i: (i, 0))

    rope = pl.BlockSpec((MIX_IN_TM, LANES), lambda i: (i % pos_blocks, 0))
    widths = [SB_WIDTH] * 3 + [RET_WIDTH]
    n_cols = 3 * SB_WIDTH + 4 * RET_WIDTH
    slabs = _slab_specs(to_bf16, steps)
    outs = pl.pallas_call(
        functools.partial(_mix_in_kernel, len(to_bf16), pos_blocks),
        grid=(steps,),
        in_specs=[pl.BlockSpec(memory_space=pltpu.SMEM), tile(D_MODEL),
                  pl.BlockSpec((1, D_MODEL), lambda i: (0, 0)),
                  _resident((D_MODEL, n_cols)), rope, rope] + slabs,
        out_specs=[tile(w) for w in widths] + slabs,
        out_shape=[jax.ShapeDtypeStruct((m, w), BF16) for w in widths]
                  + [jax.ShapeDtypeStruct(w.shape, BF16) for w in to_bf16],
        scratch_shapes=[pltpu.VMEM((D_MODEL, n_cols), BF16),
                        pltpu.VMEM((RET_HEADS, c, c), F32),
                        pltpu.VMEM((RET_HEADS, c, LANES), F32),
                        pltpu.VMEM((RET_HEADS, c, LANES), F32),
                        pltpu.VMEM((RET_HEADS, RET_KEY_DIM, LANES), F32)],
        compiler_params=pltpu.CompilerParams(
            dimension_semantics=("arbitrary",), vmem_limit_bytes=VMEM_LIMIT),
        name="mix_in",
    )(log_gamma, x2d, g, w_in, cosf, sinf, *to_bf16)
    return outs[:len(widths)], outs[len(widths):]


def _sb_kernel(dq_ref, dk_ref, eq_ref, ek_ref, q_ref, k_ref, v_ref, tri_ref, o_ref,
               qh_ref, kh_ref, vh_ref, t_ref, lsig_ref, keep_ref, tot_ref, w_ref,
               later_ref, acc_ref):
    seq = q_ref.shape[1]
    pairs = q_ref.shape[2] // LANES
    seq_tiles = seq // SB_TQ
    n_tiles = pairs * seq_tiles
    lane = lax.broadcasted_iota(jnp.int32, (SB_TQ, LANES), 1)
    head0 = lane < SB_HEAD_DIM
    row = lax.broadcasted_iota(jnp.int32, (SB_TQ, SB_TK), 0)
    col = lax.broadcasted_iota(jnp.int32, (SB_TQ, SB_TK), 1)
    rel = [col - row + c * SB_TK for c in range(SB_NB)]
    tri = tri_ref[...]

    def tile_rows(tile):
        return pl.ds(pl.multiple_of(tile * SB_TQ, SB_TQ), SB_TQ)

    for tile in range(n_tiles):
        src = (0, slice((tile % seq_tiles) * SB_TQ, (tile % seq_tiles + 1) * SB_TQ),
               slice((tile // seq_tiles) * LANES, (tile // seq_tiles + 1) * LANES))
        dst = slice(tile * SB_TQ, (tile + 1) * SB_TQ)
        q = q_ref[src]
        v = v_ref[src]
        zero = jnp.zeros_like(q)
        qh_ref[0, dst, :] = jnp.where(head0, q, zero)
        qh_ref[1, dst, :] = jnp.where(head0, zero, q)
        kh_ref[dst, :] = k_ref[src]
        vh_ref[tile, :SB_SPAN, :] = jnp.where(head0, v, zero)
        vh_ref[tile, SB_SPAN:, :] = jnp.where(head0, zero, v)
    w_ref[1] = jnp.zeros(w_ref.shape[1:], w_ref.dtype)

    def scores(hd, qt, kt):
        return lax.dot_general(qh_ref[hd, tile_rows(qt), :], kh_ref[tile_rows(kt), :],
                               (((1,), (1,)), ((), ())), preferred_element_type=F32)

    def suffix_sums(t, hd, slot, diagonal):
        sp = jnp.maximum(t, 0.0) + jnp.log2(1.0 + jnp.exp2(-jnp.abs(t)))
        lsig_ref[slot, hd] = t - sp
        if diagonal:
            sp = jnp.concatenate(
                [jnp.where(rel[c] < 0, sp[:, c * SB_TK:(c + 1) * SB_TK], 0.0)
                 for c in range(SB_NB)], axis=1)
        return sp[:, 0:1], jnp.dot(sp.astype(BF16), tri, preferred_element_type=F32)

    def stage_sums(parts, hd, slot):
        sp_first, sums = parts
        keep_ref[slot, hd] = sums
        tot_ref[slot, hd] = jnp.broadcast_to(sums[:, 0:1] - sp_first, (SB_TQ, SB_TK))

    def weights(hd, slot, qt, diagonal):
        run = None if diagonal else later_ref[qt, hd]
        for c in range(SB_NB):
            cols = slice(c * SB_TK, (c + 1) * SB_TK)
            arg = lsig_ref[slot, hd, :, cols] + keep_ref[slot, hd, :, cols]
            w = jnp.exp2(arg if run is None else arg + run)
            if diagonal:
                w = jnp.where(rel[c] < 0, w, 0.0)
            lo = hd * SB_SPAN + c * SB_TK
            w_ref[slot, :, lo:lo + SB_TK] = w.astype(BF16)
        tot = tot_ref[slot, hd]
        later_ref[qt, hd] = tot if run is None else run + tot

    def sweep(sq_ref, sk_ref, n_steps, diagonal):
        def step(s):
            return sq_ref[s + 1], sk_ref[s + 1]

        def weighted_values(s, slot):
            qt, kt = step(s)
            pv = jnp.dot(w_ref[slot], vh_ref[kt], preferred_element_type=F32)
            return qt, pv

        def accumulate(qt, pv):
            if diagonal:
                acc_ref[qt] = pv
            else:
                acc_ref[qt] += pv

        def half(s, slot):
            t = [t_ref[1 - slot, hd] for hd in range(2)]
            t_new = [scores(0, *step(s + 2))]
            sums = [suffix_sums(t[0], 0, 1 - slot, diagonal)]
            qt_prev, pv = weighted_values(s - 1, 1 - slot)
            t_new.append(scores(1, *step(s + 2)))
            sums.append(suffix_sums(t[1], 1, 1 - slot, diagonal))
            accumulate(qt_prev, pv)
            qt, _ = step(s)
            for hd in range(2):
                weights(hd, slot, qt, diagonal)
            for hd in range(2):
                t_ref[slot, hd] = t_new[hd]
            for hd in range(2):
                stage_sums(sums[hd], hd, 1 - slot)

        for hd in range(2):
            stage_sums(suffix_sums(scores(hd, *step(0)), hd, 0, diagonal), hd, 0)
        for hd in range(2):
            t_ref[1, hd] = scores(hd, *step(1))

        unroll = _sb_unroll(n_steps)

        def body(i, carry):
            for u in range(unroll):
                half(unroll * i + u, u % 2)
            return carry

        lax.fori_loop(0, n_steps // unroll, body, 0)
        accumulate(*weighted_values(n_steps - 1, 1))

    sweep(dq_ref, dk_ref, n_tiles, True)
    sweep(eq_ref, ek_ref, pairs * seq_tiles * (seq_tiles - 1) // 2, False)
    for tile in range(n_tiles):
        o_ref[0, (tile % seq_tiles) * SB_TQ:(tile % seq_tiles + 1) * SB_TQ,
              (tile // seq_tiles) * LANES:(tile // seq_tiles + 1) * LANES] = (
                  acc_ref[tile].astype(o_ref.dtype))


def _sb_unroll(n_steps):
    return max(u for u in range(2, SB_MAX_UNROLL + 1, 2) if n_steps % u == 0)


def _sb_steps(pairs, seq_tiles):
    n_tiles = pairs * seq_tiles
    diag = [(g, g) for g in range(n_tiles)]
    early = [(p * seq_tiles + t, p * seq_tiles + kt) for p in range(pairs)
             for t in range(1, seq_tiles) for kt in range(t - 1, -1, -1)]
    tables = []
    for steps in (diag, early):
        steps = [(n_tiles, 0)] + steps + [steps[-1]] * 2
        tables += [jnp.asarray([s[0] for s in steps], jnp.int32),
                   jnp.asarray([s[1] for s in steps], jnp.int32)]
    return tables


def _sb_attention(q, k, v, tri):
    b, seq, width = q.shape
    pairs = width // LANES
    seq_tiles = seq // SB_TQ
    n_tiles = pairs * seq_tiles
    spec = pl.BlockSpec((1, seq, width), lambda bi: (bi, 0, 0))
    smem = pl.BlockSpec(memory_space=pltpu.SMEM)
    return pl.pallas_call(
        _sb_kernel,
        grid=(b,),
        in_specs=[smem, smem, smem, smem, spec, spec, spec,
                  pl.BlockSpec((SB_SPAN, SB_SPAN), lambda bi: (0, 0))],
        out_specs=spec,
        out_shape=jax.ShapeDtypeStruct((b, seq, width), BF16),
        scratch_shapes=[pltpu.VMEM((2, n_tiles * SB_TQ, LANES), BF16),
                        pltpu.VMEM((n_tiles * SB_TQ, LANES), BF16),
                        pltpu.VMEM((n_tiles, 2 * SB_SPAN, LANES), BF16),
                        pltpu.VMEM((2, 2, SB_TQ, SB_SPAN), F32),
                        pltpu.VMEM((2, 2, SB_TQ, SB_SPAN), F32),
                        pltpu.VMEM((2, 2, SB_TQ, SB_SPAN), F32),
                        pltpu.VMEM((2, 2, SB_TQ, SB_TK), F32),
                        pltpu.VMEM((2, SB_TQ, 2 * SB_SPAN), BF16),
                        pltpu.VMEM((n_tiles, 2, SB_TQ, SB_TK), F32),
                        pltpu.VMEM((n_tiles + 1, SB_TQ, LANES), F32)],
        compiler_params=pltpu.CompilerParams(
            dimension_semantics=("arbitrary",), vmem_limit_bytes=SB_VMEM_LIMIT),
        name="sb_attn",
    )(*_sb_steps(pairs, seq_tiles), q, k, v, tri)


def _retention_tables(lg_ref, decay_ref, qdec_ref, kdec_ref):
    c = RET_CHUNK
    diff = (lax.broadcasted_iota(jnp.int32, (c, c), 0)
            - lax.broadcasted_iota(jnp.int32, (c, c), 1)).astype(F32)
    idx = lax.broadcasted_iota(jnp.int32, (c, LANES), 0).astype(F32)
    for h in range(RET_HEADS):
        lg = lg_ref[h]
        decay_ref[h] = jnp.where(diff >= 0, jnp.exp(jnp.maximum(diff, 0.0) * lg), 0.0)
        qdec_ref[h] = jnp.exp((idx + 1.0) * lg)
        kdec_ref[h] = jnp.exp((c - 1.0 - idx) * lg)


def _retention_chunk(q, k, v, gate, lg_ref, decay_ref, qdec_ref, kdec_ref, state_ref):
    c = RET_CHUNK
    heads = range(RET_HEADS)
    q, k, v = ([a[:, h * LANES:(h + 1) * LANES] for h in heads] for a in (q, k, v))
    state = [state_ref[h] for h in heads]
    scores = [lax.dot_general(q[h], k[h], (((1,), (1,)), ((), ())),
                              preferred_element_type=F32) for h in heads]
    cross = [jnp.dot((q[h].astype(F32) * qdec_ref[h]).astype(BF16),
                     state[h].astype(BF16), preferred_element_type=F32)
             for h in heads]
    kd_t = [(k[h].astype(F32) * kdec_ref[h]).T.astype(BF16) for h in heads]
    kv = [jnp.dot(kd_t[h], v[h], preferred_element_type=F32) for h in heads]
    inner = [jnp.dot((scores[h] * decay_ref[h]).astype(BF16), v[h],
                     preferred_element_type=F32) for h in heads]
    for h in heads:
        chunk_decay = jnp.exp(jnp.full((1, LANES), float(c), F32) * lg_ref[h])
        state_ref[h] = chunk_decay * state[h] + kv[h]
    out = []
    for h in heads:
        o = inner[h] + cross[h]
        mu = jnp.mean(o, axis=-1, keepdims=True)
        ctr = o - mu
        var = jnp.mean(ctr * ctr, axis=-1, keepdims=True)
        g = gate[:, h * LANES:(h + 1) * LANES]
        out.append((ctr * lax.rsqrt(var + GN_EPS) * (g * jax.nn.sigmoid(g))).astype(BF16))
    return jnp.concatenate(out, axis=1)


def _merge_kernel(n_gate_blocks, x_ref, osb_ref, or_ref, gpre_ref, *refs):
    gate32_refs, refs = refs[:n_gate_blocks], refs[n_gate_blocks:]
    (wsb32_ref, wret32_ref, wout32_ref, gpost_ref, o_ref,
     wgate_ref, wsb_ref, wret_ref, wout_ref) = refs

    @pl.when(pl.program_id(0) == 0)
    def _():
        width = gate32_refs[0].shape[1]
        for b, ref in enumerate(gate32_refs):
            wgate_ref[:, b * width:(b + 1) * width] = ref[...].astype(BF16)
        wsb_ref[...] = wsb32_ref[...].astype(BF16)
        wret_ref[...] = wret32_ref[...].astype(BF16)
        wout_ref[...] = wout32_ref[...].astype(BF16)

    for r in range(MIX_TM // ROW_GROUP):
        rows = slice(r * ROW_GROUP, (r + 1) * ROW_GROUP)
        x = x_ref[rows, :]
        p_sb = jnp.dot(osb_ref[rows, :], wsb_ref[...], preferred_element_type=F32)
        p_ret = jnp.dot(or_ref[rows, :], wret_ref[...], preferred_element_type=F32)
        h = _rms(x, gpre_ref[...]).astype(BF16)
        gate_sb = jnp.dot(h, wgate_ref[:, :D_MODEL], preferred_element_type=F32)
        merged = jax.nn.sigmoid(gate_sb) * p_sb
        gate_ret = jnp.dot(h, wgate_ref[:, D_MODEL:], preferred_element_type=F32)
        merged = merged + jax.nn.sigmoid(gate_ret) * p_ret
        y = jnp.dot(merged.astype(BF16), wout_ref[...], preferred_element_type=F32)
        o_ref[rows, :] = x + _rms(y, gpost_ref[...])


def _merge(x2d, o_sb, o_r, g_pre, w_in, w_sb, w_ret, w_out, g_post):
    m = x2d.shape[0]
    gate_lo = w_in.shape[1] - 2 * D_MODEL
    gate_block = math.gcd(gate_lo, 2 * D_MODEL)
    n_gate_blocks = 2 * D_MODEL // gate_block

    def tile(width):
        return pl.BlockSpec((MIX_TM, width), lambda i: (i, 0))

    def gate_cols(b):
        return pl.BlockSpec((D_MODEL, gate_block),
                            lambda i: (0, gate_lo // gate_block + b),
                            pipeline_mode=pl.Buffered(1))

    gain = pl.BlockSpec((1, D_MODEL), lambda i: (0, 0))
    return pl.pallas_call(
        functools.partial(_merge_kernel, n_gate_blocks),
        grid=(m // MIX_TM,),
        in_specs=[tile(D_MODEL), tile(SB_WIDTH), tile(RET_WIDTH), gain]
                 + [gate_cols(b) for b in range(n_gate_blocks)]
                 + [_resident((SB_WIDTH, D_MODEL)), _resident((RET_WIDTH, D_MODEL)),
                    _resident((D_MODEL, D_MODEL)), gain],
        out_specs=tile(D_MODEL),
        out_shape=jax.ShapeDtypeStruct((m, D_MODEL), F32),
        scratch_shapes=[pltpu.VMEM((D_MODEL, 2 * D_MODEL), BF16),
                        pltpu.VMEM((SB_WIDTH, D_MODEL), BF16),
                        pltpu.VMEM((RET_WIDTH, D_MODEL), BF16),
                        pltpu.VMEM((D_MODEL, D_MODEL), BF16)],
        compiler_params=pltpu.CompilerParams(
            dimension_semantics=("arbitrary",), vmem_limit_bytes=MERGE_VMEM_LIMIT),
        name="merge",
    )(x2d, o_sb, o_r, g_pre, *([w_in] * n_gate_blocks), w_sb, w_ret, w_out, g_post)


def _rope_tables(seq):
    half = RET_KEY_DIM // 2
    pos = np.arange(seq, dtype=np.float64)
    inv_freq = ROPE_BASE ** (-np.arange(half, dtype=np.float64) / half)
    ang = pos[:, None] * inv_freq[None, :]
    cos, sin = np.cos(ang), np.sin(ang)
    return (jnp.asarray(np.concatenate([cos, cos], axis=-1), dtype=F32),
            jnp.asarray(np.concatenate([-sin, sin], axis=-1), dtype=F32))


def _suffix_sum_matrix():
    j = np.arange(SB_SPAN)[:, None]
    s = np.arange(SB_SPAN)[None, :]
    return jnp.asarray(-(j > s).astype(np.float32), dtype=BF16)


def kernel(x, g_ffn1_pre, g_ffn1_post, w_ffn1_gate, w_ffn1_up, w_ffn1_down,
           g_mix_pre, w_in, w_proj_sb, w_proj_ret, w_out, g_mix_post,
           g_ffn2_pre, g_ffn2_post, w_ffn2_gate, w_ffn2_up, w_ffn2_down):
    b, seq, d = x.shape
    depth = w_in.shape[0]
    log_gamma = jnp.log(1.0 - 2.0 ** (-5.0 - jnp.arange(RET_HEADS, dtype=F32)))
    cosf, sinf = _rope_tables(seq)
    tri = _suffix_sum_matrix()
    x2d = x.reshape(b * seq, d)
    for l in range(depth):
        x2d = _ffn(x2d, g_ffn1_pre[l][None], g_ffn1_post[l][None],
                   w_ffn1_gate[l].astype(BF16), w_ffn1_up[l].astype(BF16),
                   w_ffn1_down[l].astype(BF16))
        (q_sb, k_sb, v_sb, o_r), (w2_gate, w2_up, w2_down) = _mix_in(
            x2d, g_mix_pre[l][None], w_in[l], cosf, sinf, log_gamma, seq,
            to_bf16=(w_ffn2_gate[l], w_ffn2_up[l], w_ffn2_down[l]))
        shp = (b, seq, SB_WIDTH)
        o_sb = _sb_attention(q_sb.reshape(shp), k_sb.reshape(shp),
                             v_sb.reshape(shp), tri)
        x2d = _merge(x2d, o_sb.reshape(b * seq, SB_WIDTH), o_r,
                     g_mix_pre[l][None], w_in[l], w_proj_sb[l], w_proj_ret[l],
                     w_out[l], g_mix_post[l][None])
        x2d = _ffn(x2d, g_ffn2_pre[l][None], g_ffn2_post[l][None],
                   w2_gate, w2_up, w2_down)
    return x2d.reshape(b, seq, d)
```

```python
import functools
import math

import numpy as np
import jax
import jax.numpy as jnp
from jax import lax
from jax.experimental import pallas as pl
from jax.experimental.pallas import tpu as pltpu

F32 = jnp.float32
BF16 = jnp.bfloat16

D_MODEL = 1024
D_FF = 4 * D_MODEL
SB_HEADS = 8
SB_HEAD_DIM = 64
SB_WIDTH = SB_HEADS * SB_HEAD_DIM
RET_HEADS = 4
RET_KEY_DIM = 128
RET_WIDTH = RET_HEADS * RET_KEY_DIM
RMS_EPS = 1e-6
GN_EPS = 1e-5
ROPE_BASE = 10000.0

LANES = 128
BF16_SUBLANES = 16
VMEM_LIMIT = 56 * 1024 * 1024
SB_VMEM_LIMIT = 48 * 1024 * 1024
MERGE_VMEM_LIMIT = 50 * 1024 * 1024

FFN_TM = 1024
FFN_TF = 1024
MIX_IN_TM = 1024
MIX_TM = 1024
ROW_GROUP = 256
SB_TQ = 256
SB_TK = LANES
SB_SPAN = SB_TQ
SB_NB = SB_SPAN // SB_TK
SB_MAX_UNROLL = 28
SB_Q_SCALE = SB_HEAD_DIM ** -0.5 * math.log2(math.e)
RET_CHUNK = ROW_GROUP


def _rms(x, g):
    ms = jnp.mean(x * x, axis=-1, keepdims=True)
    return x * lax.rsqrt(ms + RMS_EPS) * g


def _resident(shape):
    nd = len(shape)
    return pl.BlockSpec(shape, lambda *_: (0,) * nd, pipeline_mode=pl.Buffered(1))


def _cast_slabs(srcs, dsts):
    for src, dst in zip(srcs, dsts):
        dst[...] = src[...].astype(dst.dtype)


def _slab_specs(arrays, steps):
    assert all(w.shape[0] % (steps * BF16_SUBLANES) == 0 for w in arrays)
    return [pl.BlockSpec((w.shape[0] // steps, w.shape[1]), lambda i: (i, 0))
            for w in arrays]


def _ffn_kernel(x_ref, gpre_ref, gpost_ref, wg_ref, wu_ref, wd_ref, o_ref):
    groups = x_ref.shape[0] // ROW_GROUP
    rows = [slice(r * ROW_GROUP, (r + 1) * ROW_GROUP) for r in range(groups)]
    x = [x_ref[r, :] for r in rows]
    h = [_rms(xr, gpre_ref[...]).astype(BF16) for xr in x]
    acc = [None] * groups
    for j in range(D_FF // FFN_TF):
        cols = slice(j * FFN_TF, (j + 1) * FFN_TF)
        for r in range(groups):
            g = jnp.dot(h[r], wg_ref[:, cols], preferred_element_type=F32)
            u = jnp.dot(h[r], wu_ref[:, cols], preferred_element_type=F32)
            a = (g * jax.nn.sigmoid(g) * u).astype(BF16)
            d = jnp.dot(a, wd_ref[cols, :], preferred_element_type=F32)
            acc[r] = d if acc[r] is None else acc[r] + d
    for r in range(groups):
        o_ref[rows[r], :] = x[r] + 0.5 * _rms(acc[r], gpost_ref[...])


def _ffn(x2d, g_pre, g_post, wg, wu, wd):
    m = x2d.shape[0]
    tile = pl.BlockSpec((FFN_TM, D_MODEL), lambda i: (i, 0))
    gain = pl.BlockSpec((1, D_MODEL), lambda i: (0, 0))
    return pl.pallas_call(
        _ffn_kernel,
        grid=(m // FFN_TM,),
        in_specs=[tile, gain, gain,
                  _resident((D_MODEL, D_FF)), _resident((D_MODEL, D_FF)),
                  _resident((D_FF, D_MODEL))],
        out_specs=tile,
        out_shape=jax.ShapeDtypeStruct((m, D_MODEL), F32),
        compiler_params=pltpu.CompilerParams(
            dimension_semantics=("parallel",), vmem_limit_bytes=VMEM_LIMIT),
        name="ffn",
    )(x2d, g_pre, g_post, wg, wu, wd)

def _mix_in_kernel(n_casts, pos_blocks, lg_ref, x_ref, g_ref, w32_ref, cos_ref, sin_ref,
                   *refs):
    cast_in, refs = refs[:n_casts], refs[n_casts:]
    qsb_ref, ksb_ref, vsb_ref, or_ref = refs[:4]
    cast_out = refs[4:4 + n_casts]
    w_ref, decay_ref, qdec_ref, kdec_ref, state_ref = refs[4 + n_casts:]

    @pl.when(pl.program_id(0) == 0)
    def _():
        w_ref[...] = w32_ref[...].astype(BF16)
        _retention_tables(lg_ref, decay_ref, qdec_ref, kdec_ref)

    @pl.when(pl.program_id(0) % pos_blocks == 0)
    def _():
        state_ref[...] = jnp.zeros(state_ref.shape, F32)

    _cast_slabs(cast_in, cast_out)
    for r in range(MIX_IN_TM // ROW_GROUP):
        rows = slice(r * ROW_GROUP, (r + 1) * ROW_GROUP)
        h = _rms(x_ref[rows, :], g_ref[...]).astype(BF16)

        def mm(lo, width):
            return jnp.dot(h, w_ref[:, lo:lo + width], preferred_element_type=F32)

        def rotary(t):
            cosf = cos_ref[rows, :]
            sinf = sin_ref[rows, :]
            parts = []
            for hd in range(RET_HEADS):
                th = t[:, hd * LANES:(hd + 1) * LANES]
                parts.append(th * cosf + pltpu.roll(th, LANES // 2, axis=1) * sinf)
            return jnp.concatenate(parts, axis=1)

        c = 3 * SB_WIDTH
        q_r = rotary(mm(c, RET_WIDTH)).astype(BF16); c += RET_WIDTH
        k_r = (rotary(mm(c, RET_WIDTH)) * (RET_KEY_DIM ** -0.5)).astype(BF16); c += RET_WIDTH
        v_r = mm(c, RET_WIDTH).astype(BF16); c += RET_WIDTH
        g_r = mm(c, RET_WIDTH)
        or_ref[rows, :] = _retention_chunk(q_r, k_r, v_r, g_r, lg_ref, decay_ref,
                                           qdec_ref, kdec_ref, state_ref)
        c = 0
        qsb_ref[rows, :] = (mm(c, SB_WIDTH) * SB_Q_SCALE).astype(BF16); c += SB_WIDTH
        ksb_ref[rows, :] = mm(c, SB_WIDTH).astype(BF16); c += SB_WIDTH
        vsb_ref[rows, :] = mm(c, SB_WIDTH).astype(BF16)


def _mix_in(x2d, g, w_in, cosf, sinf, log_gamma, seq, to_bf16=()):
    m = x2d.shape[0]
    steps = m // MIX_IN_TM
    pos_blocks = seq // MIX_IN_TM
    c = RET_CHUNK

    def tile(width):
        return pl.BlockSpec((MIX_IN_TM, width), lambda i: (i, 0))

    rope = pl.BlockSpec((MIX_IN_TM, LANES), lambda i: (i % pos_blocks, 0))
    widths = [SB_WIDTH] * 3 + [RET_WIDTH]
    n_cols = 3 * SB_WIDTH + 4 * RET_WIDTH
    slabs = _slab_specs(to_bf16, steps)
    outs = pl.pallas_call(
        functools.partial(_mix_in_kernel, len(to_bf16), pos_blocks),
        grid=(steps,),
        in_specs=[pl.BlockSpec(memory_space=pltpu.SMEM), tile(D_MODEL),
                  pl.BlockSpec((1, D_MODEL), lambda i: (0, 0)),
                  _resident((D_MODEL, n_cols)), rope, rope] + slabs,
        out_specs=[tile(w) for w in widths] + slabs,
        out_shape=[jax.ShapeDtypeStruct((m, w), BF16) for w in widths]
                  + [jax.ShapeDtypeStruct(w.shape, BF16) for w in to_bf16],
        scratch_shapes=[pltpu.VMEM((D_MODEL, n_cols), BF16),
                        pltpu.VMEM((RET_HEADS, c, c), F32),
                        pltpu.VMEM((RET_HEADS, c, LANES), F32),
                        pltpu.VMEM((RET_HEADS, c, LANES), F32),
                        pltpu.VMEM((RET_HEADS, RET_KEY_DIM, LANES), F32)],
        compiler_params=pltpu.CompilerParams(
            dimension_semantics=("arbitrary",), vmem_limit_bytes=VMEM_LIMIT),
        name="mix_in",
    )(log_gamma, x2d, g, w_in, cosf, sinf, *to_bf16)
    return outs[:len(widths)], outs[len(widths):]


def _sb_kernel(dq_ref, dk_ref, eq_ref, ek_ref, q_ref, k_ref, v_ref, tri_ref, o_ref,
               qh_ref, kh_ref, vh_ref, t_ref, lsig_ref, keep_ref, tot_ref, w_ref,
               later_ref, acc_ref):
    seq = q_ref.shape[1]
    pairs = q_ref.shape[2] // LANES
    seq_tiles = seq // SB_TQ
    n_tiles = pairs * seq_tiles
    lane = lax.broadcasted_iota(jnp.int32, (SB_TQ, LANES), 1)
    head0 = lane < SB_HEAD_DIM
    row = lax.broadcasted_iota(jnp.int32, (SB_TQ, SB_TK), 0)
    col = lax.broadcasted_iota(jnp.int32, (SB_TQ, SB_TK), 1)
    rel = [col - row + c * SB_TK for c in range(SB_NB)]
    tri = tri_ref[...]

    def tile_rows(tile):
        return pl.ds(pl.multiple_of(tile * SB_TQ, SB_TQ), SB_TQ)

    for tile in range(n_tiles):
        src = (0, slice((tile % seq_tiles) * SB_TQ, (tile % seq_tiles + 1) * SB_TQ),
               slice((tile // seq_tiles) * LANES, (tile // seq_tiles + 1) * LANES))
        dst = slice(tile * SB_TQ, (tile + 1) * SB_TQ)
        q = q_ref[src]
        v = v_ref[src]
        zero = jnp.zeros_like(q)
        qh_ref[0, dst, :] = jnp.where(head0, q, zero)
        qh_ref[1, dst, :] = jnp.where(head0, zero, q)
        kh_ref[dst, :] = k_ref[src]
        vh_ref[tile, :SB_SPAN, :] = jnp.where(head0, v, zero)
        vh_ref[tile, SB_SPAN:, :] = jnp.where(head0, zero, v)
    w_ref[1] = jnp.zeros(w_ref.shape[1:], w_ref.dtype)

    def scores(hd, qt, kt):
        return lax.dot_general(qh_ref[hd, tile_rows(qt), :], kh_ref[tile_rows(kt), :],
                               (((1,), (1,)), ((), ())), preferred_element_type=F32)

    def suffix_sums(t, hd, slot, diagonal):
        sp = jnp.maximum(t, 0.0) + jnp.log2(1.0 + jnp.exp2(-jnp.abs(t)))
        lsig_ref[slot, hd] = t - sp
        if diagonal:
            sp = jnp.concatenate(
                [jnp.where(rel[c] < 0, sp[:, c * SB_TK:(c + 1) * SB_TK], 0.0)
                 for c in range(SB_NB)], axis=1)
        return sp[:, 0:1], jnp.dot(sp.astype(BF16), tri, preferred_element_type=F32)

    def stage_sums(parts, hd, slot):
        sp_first, sums = parts
        keep_ref[slot, hd] = sums
        tot_ref[slot, hd] = jnp.broadcast_to(sums[:, 0:1] - sp_first, (SB_TQ, SB_TK))

    def weights(hd, slot, qt, diagonal):
        run = None if diagonal else later_ref[qt, hd]
        for c in range(SB_NB):
            cols = slice(c * SB_TK, (c + 1) * SB_TK)
            arg = lsig_ref[slot, hd, :, cols] + keep_ref[slot, hd, :, cols]
            w = jnp.exp2(arg if run is None else arg + run)
            if diagonal:
                w = jnp.where(rel[c] < 0, w, 0.0)
            lo = hd * SB_SPAN + c * SB_TK
            w_ref[slot, :, lo:lo + SB_TK] = w.astype(BF16)
        tot = tot_ref[slot, hd]
        later_ref[qt, hd] = tot if run is None else run + tot

    def sweep(sq_ref, sk_ref, n_steps, diagonal):
        def step(s):
            return sq_ref[s + 1], sk_ref[s + 1]

        def weighted_values(s, slot):
            qt, kt = step(s)
            pv = jnp.dot(w_ref[slot], vh_ref[kt], preferred_element_type=F32)
            return qt, pv

        def accumulate(qt, pv):
            if diagonal:
                acc_ref[qt] = pv
            else:
                acc_ref[qt] += pv

        def half(s, slot):
            t = [t_ref[1 - slot, hd] for hd in range(2)]
            t_new = [scores(0, *step(s + 2))]
            sums = [suffix_sums(t[0], 0, 1 - slot, diagonal)]
            qt_prev, pv = weighted_values(s - 1, 1 - slot)
            t_new.append(scores(1, *step(s + 2)))
            sums.append(suffix_sums(t[1], 1, 1 - slot, diagonal))
            accumulate(qt_prev, pv)
            qt, _ = step(s)
            for hd in range(2):
                weights(hd, slot, qt, diagonal)
            for hd in range(2):
                t_ref[slot, hd] = t_new[hd]
            for hd in range(2):
                stage_sums(sums[hd], hd, 1 - slot)

        for hd in range(2):
            stage_sums(suffix_sums(scores(hd, *step(0)), hd, 0, diagonal), hd, 0)
        for hd in range(2):
            t_ref[1, hd] = scores(hd, *step(1))

        unroll = _sb_unroll(n_steps)

        def body(i, carry):
            for u in range(unroll):
                half(unroll * i + u, u % 2)
            return carry

        lax.fori_loop(0, n_steps // unroll, body, 0)
        accumulate(*weighted_values(n_steps - 1, 1))

    sweep(dq_ref, dk_ref, n_tiles, True)
    sweep(eq_ref, ek_ref, pairs * seq_tiles * (seq_tiles - 1) // 2, False)
    for tile in range(n_tiles):
        o_ref[0, (tile % seq_tiles) * SB_TQ:(tile % seq_tiles + 1) * SB_TQ,
              (tile // seq_tiles) * LANES:(tile // seq_tiles + 1) * LANES] = (
                  acc_ref[tile].astype(o_ref.dtype))


def _sb_unroll(n_steps):
    return max(u for u in range(2, SB_MAX_UNROLL + 1, 2) if n_steps % u == 0)


def _sb_steps(pairs, seq_tiles):
    n_tiles = pairs * seq_tiles
    diag = [(g, g) for g in range(n_tiles)]
    early = [(p * seq_tiles + t, p * seq_tiles + kt) for p in range(pairs)
             for t in range(1, seq_tiles) for kt in range(t - 1, -1, -1)]
    tables = []
    for steps in (diag, early):
        steps = [(n_tiles, 0)] + steps + [steps[-1]] * 2
        tables += [jnp.asarray([s[0] for s in steps], jnp.int32),
                   jnp.asarray([s[1] for s in steps], jnp.int32)]
    return tables


def _sb_attention(q, k, v, tri):
    b, seq, width = q.shape
    pairs = width // LANES
    seq_tiles = seq // SB_TQ
    n_tiles = pairs * seq_tiles
    spec = pl.BlockSpec((1, seq, width), lambda bi: (bi, 0, 0))
    smem = pl.BlockSpec(memory_space=pltpu.SMEM)
    return pl.pallas_call(
        _sb_kernel,
        grid=(b,),
        in_specs=[smem, smem, smem, smem, spec, spec, spec,
                  pl.BlockSpec((SB_SPAN, SB_SPAN), lambda bi: (0, 0))],
        out_specs=spec,
        out_shape=jax.ShapeDtypeStruct((b, seq, width), BF16),
        scratch_shapes=[pltpu.VMEM((2, n_tiles * SB_TQ, LANES), BF16),
                        pltpu.VMEM((n_tiles * SB_TQ, LANES), BF16),
                        pltpu.VMEM((n_tiles, 2 * SB_SPAN, LANES), BF16),
                        pltpu.VMEM((2, 2, SB_TQ, SB_SPAN), F32),
                        pltpu.VMEM((2, 2, SB_TQ, SB_SPAN), F32),
                        pltpu.VMEM((2, 2, SB_TQ, SB_SPAN), F32),
                        pltpu.VMEM((2, 2, SB_TQ, SB_TK), F32),
                        pltpu.VMEM((2, SB_TQ, 2 * SB_SPAN), BF16),
                        pltpu.VMEM((n_tiles, 2, SB_TQ, SB_TK), F32),
                        pltpu.VMEM((n_tiles + 1, SB_TQ, LANES), F32)],
        compiler_params=pltpu.CompilerParams(
            dimension_semantics=("arbitrary",), vmem_limit_bytes=SB_VMEM_LIMIT),
        name="sb_attn",
    )(*_sb_steps(pairs, seq_tiles), q, k, v, tri)


def _retention_tables(lg_ref, decay_ref, qdec_ref, kdec_ref):
    c = RET_CHUNK
    diff = (lax.broadcasted_iota(jnp.int32, (c, c), 0)
            - lax.broadcasted_iota(jnp.int32, (c, c), 1)).astype(F32)
    idx = lax.broadcasted_iota(jnp.int32, (c, LANES), 0).astype(F32)
    for h in range(RET_HEADS):
        lg = lg_ref[h]
        decay_ref[h] = jnp.where(diff >= 0, jnp.exp(jnp.maximum(diff, 0.0) * lg), 0.0)
        qdec_ref[h] = jnp.exp((idx + 1.0) * lg)
        kdec_ref[h] = jnp.exp((c - 1.0 - idx) * lg)


def _retention_chunk(q, k, v, gate, lg_ref, decay_ref, qdec_ref, kdec_ref, state_ref):
    c = RET_CHUNK
    heads = range(RET_HEADS)
    q, k, v = ([a[:, h * LANES:(h + 1) * LANES] for h in heads] for a in (q, k, v))
    state = [state_ref[h] for h in heads]
    scores = [lax.dot_general(q[h], k[h], (((1,), (1,)), ((), ())),
                              preferred_element_type=F32) for h in heads]
    cross = [jnp.dot((q[h].astype(F32) * qdec_ref[h]).astype(BF16),
                     state[h].astype(BF16), preferred_element_type=F32)
             for h in heads]
    kd_t = [(k[h].astype(F32) * kdec_ref[h]).T.astype(BF16) for h in heads]
    kv = [jnp.dot(kd_t[h], v[h], preferred_element_type=F32) for h in heads]
    inner = [jnp.dot((scores[h] * decay_ref[h]).astype(BF16), v[h],
                     preferred_element_type=F32) for h in heads]
    for h in heads:
        chunk_decay = jnp.exp(jnp.full((1, LANES), float(c), F32) * lg_ref[h])
        state_ref[h] = chunk_decay * state[h] + kv[h]
    out = []
    for h in heads:
        o = inner[h] + cross[h]
        mu = jnp.mean(o, axis=-1, keepdims=True)
        ctr = o - mu
        var = jnp.mean(ctr * ctr, axis=-1, keepdims=True)
        g = gate[:, h * LANES:(h + 1) * LANES]
        out.append((ctr * lax.rsqrt(var + GN_EPS) * (g * jax.nn.sigmoid(g))).astype(BF16))
    return jnp.concatenate(out, axis=1)


def _merge_kernel(n_gate_blocks, x_ref, osb_ref, or_ref, gpre_ref, *refs):
    gate32_refs, refs = refs[:n_gate_blocks], refs[n_gate_blocks:]
    (wsb32_ref, wret32_ref, wout32_ref, gpost_ref, o_ref,
     wgate_ref, wsb_ref, wret_ref, wout_ref) = refs

    @pl.when(pl.program_id(0) == 0)
    def _():
        width = gate32_refs[0].shape[1]
        for b, ref in enumerate(gate32_refs):
            wgate_ref[:, b * width:(b + 1) * width] = ref[...].astype(BF16)
        wsb_ref[...] = wsb32_ref[...].astype(BF16)
        wret_ref[...] = wret32_ref[...].astype(BF16)
        wout_ref[...] = wout32_ref[...].astype(BF16)

    for r in range(MIX_TM // ROW_GROUP):
        rows = slice(r * ROW_GROUP, (r + 1) * ROW_GROUP)
        x = x_ref[rows, :]
        p_sb = jnp.dot(osb_ref[rows, :], wsb_ref[...], preferred_element_type=F32)
        p_ret = jnp.dot(or_ref[rows, :], wret_ref[...], preferred_element_type=F32)
        h = _rms(x, gpre_ref[...]).astype(BF16)
        gate_sb = jnp.dot(h, wgate_ref[:, :D_MODEL], preferred_element_type=F32)
        merged = jax.nn.sigmoid(gate_sb) * p_sb
        gate_ret = jnp.dot(h, wgate_ref[:, D_MODEL:], preferred_element_type=F32)
        merged = merged + jax.nn.sigmoid(gate_ret) * p_ret
        y = jnp.dot(merged.astype(BF16), wout_ref[...], preferred_element_type=F32)
        o_ref[rows, :] = x + _rms(y, gpost_ref[...])


def _merge(x2d, o_sb, o_r, g_pre, w_in, w_sb, w_ret, w_out, g_post):
    m = x2d.shape[0]
    gate_lo = w_in.shape[1] - 2 * D_MODEL
    gate_block = math.gcd(gate_lo, 2 * D_MODEL)
    n_gate_blocks = 2 * D_MODEL // gate_block

    def tile(width):
        return pl.BlockSpec((MIX_TM, width), lambda i: (i, 0))

    def gate_cols(b):
        return pl.BlockSpec((D_MODEL, gate_block),
                            lambda i: (0, gate_lo // gate_block + b),
                            pipeline_mode=pl.Buffered(1))

    gain = pl.BlockSpec((1, D_MODEL), lambda i: (0, 0))
    return pl.pallas_call(
        functools.partial(_merge_kernel, n_gate_blocks),
        grid=(m // MIX_TM,),
        in_specs=[tile(D_MODEL), tile(SB_WIDTH), tile(RET_WIDTH), gain]
                 + [gate_cols(b) for b in range(n_gate_blocks)]
                 + [_resident((SB_WIDTH, D_MODEL)), _resident((RET_WIDTH, D_MODEL)),
                    _resident((D_MODEL, D_MODEL)), gain],
        out_specs=tile(D_MODEL),
        out_shape=jax.ShapeDtypeStruct((m, D_MODEL), F32),
        scratch_shapes=[pltpu.VMEM((D_MODEL, 2 * D_MODEL), BF16),
                        pltpu.VMEM((SB_WIDTH, D_MODEL), BF16),
                        pltpu.VMEM((RET_WIDTH, D_MODEL), BF16),
                        pltpu.VMEM((D_MODEL, D_MODEL), BF16)],
        compiler_params=pltpu.CompilerParams(
            dimension_semantics=("arbitrary",), vmem_limit_bytes=MERGE_VMEM_LIMIT),
        name="merge",
    )(x2d, o_sb, o_r, g_pre, *([w_in] * n_gate_blocks), w_sb, w_ret, w_out, g_post)


def _rope_tables(seq):
    half = RET_KEY_DIM // 2
    pos = np.arange(seq, dtype=np.float64)
    inv_freq = ROPE_BASE ** (-np.arange(half, dtype=np.float64) / half)
    ang = pos[:, None] * inv_freq[None, :]
    cos, sin = np.cos(ang), np.sin(ang)
    return (jnp.asarray(np.concatenate([cos, cos], axis=-1), dtype=F32),
            jnp.asarray(np.concatenate([-sin, sin], axis=-1), dtype=F32))


def _suffix_sum_matrix():
    j = np.arange(SB_SPAN)[:, None]
    s = np.arange(SB_SPAN)[None, :]
    return jnp.asarray(-(j > s).astype(np.float32), dtype=BF16)


def kernel(x, g_ffn1_pre, g_ffn1_post, w_ffn1_gate, w_ffn1_up, w_ffn1_down,
           g_mix_pre, w_in, w_proj_sb, w_proj_ret, w_out, g_mix_post,
           g_ffn2_pre, g_ffn2_post, w_ffn2_gate, w_ffn2_up, w_ffn2_down):
    b, seq, d = x.shape
    depth = w_in.shape[0]
    log_gamma = jnp.log(1.0 - 2.0 ** (-5.0 - jnp.arange(RET_HEADS, dtype=F32)))
    cosf, sinf = _rope_tables(seq)
    tri = _suffix_sum_matrix()
    x2d = x.reshape(b * seq, d)
    for l in range(depth):
        x2d = _ffn(x2d, g_ffn1_pre[l][None], g_ffn1_post[l][None],
                   w_ffn1_gate[l].astype(BF16), w_ffn1_up[l].astype(BF16),
                   w_ffn1_down[l].astype(BF16))
        (q_sb, k_sb, v_sb, o_r), (w2_gate, w2_up, w2_down) = _mix_in(
            x2d, g_mix_pre[l][None], w_in[l], cosf, sinf, log_gamma, seq,
            to_bf16=(w_ffn2_gate[l], w_ffn2_up[l], w_ffn2_down[l]))
        shp = (b, seq, SB_WIDTH)
        o_sb = _sb_attention(q_sb.reshape(shp), k_sb.reshape(shp),
                             v_sb.reshape(shp), tri)
        x2d = _merge(x2d, o_sb.reshape(b * seq, SB_WIDTH), o_r,
                     g_mix_pre[l][None], w_in[l], w_proj_sb[l], w_proj_ret[l],
                     w_out[l], g_mix_post[l][None])
        x2d = _ffn(x2d, g_ffn2_pre[l][None], g_ffn2_post[l][None],
                   w2_gate, w2_up, w2_down)
    return x2d.reshape(b, seq, d)
```

```python
import functools
import math

import numpy as np
import jax
import jax.numpy as jnp
from jax import lax
from jax.experimental import pallas as pl
from jax.experimental.pallas import tpu as pltpu

F32 = jnp.float32
BF16 = jnp.bfloat16

D_MODEL = 1024
D_FF = 4 * D_MODEL
SB_HEADS = 8
SB_HEAD_DIM = 64
SB_WIDTH = SB_HEADS * SB_HEAD_DIM
RET_HEADS = 4
RET_KEY_DIM = 128
RET_WIDTH = RET_HEADS * RET_KEY_DIM
RMS_EPS = 1e-6
GN_EPS = 1e-5
ROPE_BASE = 10000.0

LANES = 128
BF16_SUBLANES = 16
VMEM_LIMIT = 56 * 1024 * 1024
SB_VMEM_LIMIT = 48 * 1024 * 1024
MERGE_VMEM_LIMIT = 50 * 1024 * 1024

FFN_TM = 1024
FFN_TF = 1024
MIX_IN_TM = 1024
MIX_TM = 1024
ROW_GROUP = 256
SB_TQ = 256
SB_TK = LANES
SB_SPAN = SB_TQ
SB_NB = SB_SPAN // SB_TK
SB_MAX_UNROLL = 28
SB_Q_SCALE = SB_HEAD_DIM ** -0.5 * math.log2(math.e)
RET_CHUNK = ROW_GROUP


def _rms(x, g):
    ms = jnp.mean(x * x, axis=-1, keepdims=True)
    return x * lax.rsqrt(ms + RMS_EPS) * g


def _resident(shape):
    nd = len(shape)
    return pl.BlockSpec(shape, lambda *_: (0,) * nd, pipeline_mode=pl.Buffered(1))


def _cast_slabs(srcs, dsts):
    for src, dst in zip(srcs, dsts):
        dst[...] = src[...].astype(dst.dtype)


def _slab_specs(arrays, steps):
    assert all(w.shape[0] % (steps * BF16_SUBLANES) == 0 for w in arrays)
    return [pl.BlockSpec((w.shape[0] // steps, w.shape[1]), lambda i: (i, 0))
            for w in arrays]


def _ffn_kernel(x_ref, gpre_ref, gpost_ref, wg_ref, wu_ref, wd_ref, o_ref):
    groups = x_ref.shape[0] // ROW_GROUP
    rows = [slice(r * ROW_GROUP, (r + 1) * ROW_GROUP) for r in range(groups)]
    x = [x_ref[r, :] for r in rows]
    h = [_rms(xr, gpre_ref[...]).astype(BF16) for xr in x]
    acc = [None] * groups
    for j in range(D_FF // FFN_TF):
        cols = slice(j * FFN_TF, (j + 1) * FFN_TF)
        for r in range(groups):
            g = jnp.dot(h[r], wg_ref[:, cols], preferred_element_type=F32)
            u = jnp.dot(h[r], wu_ref[:, cols], preferred_element_type=F32)
            a = (g * jax.nn.sigmoid(g) * u).astype(BF16)
            d = jnp.dot(a, wd_ref[cols, :], preferred_element_type=F32)
            acc[r] = d if acc[r] is None else acc[r] + d
    for r in range(groups):
        o_ref[rows[r], :] = x[r] + 0.5 * _rms(acc[r], gpost_ref[...])


def _ffn(x2d, g_pre, g_post, wg, wu, wd):
    m = x2d.shape[0]
    tile = pl.BlockSpec((FFN_TM, D_MODEL), lambda i: (i, 0))
    gain = pl.BlockSpec((1, D_MODEL), lambda i: (0, 0))
    return pl.pallas_call(
        _ffn_kernel,
        grid=(m // FFN_TM,),
        in_specs=[tile, gain, gain,
                  _resident((D_MODEL, D_FF)), _resident((D_MODEL, D_FF)),
                  _resident((D_FF, D_MODEL))],
        out_specs=tile,
        out_shape=jax.ShapeDtypeStruct((m, D_MODEL), F32),
        compiler_params=pltpu.CompilerParams(
            dimension_semantics=("parallel",), vmem_limit_bytes=VMEM_LIMIT,
            allow_input_fusion=[False, False, False, True, True, True]),
        name="ffn",
    )(x2d, g_pre, g_post, wg, wu, wd)

def _mix_in_kernel(n_casts, pos_blocks, lg_ref, x_ref, g_ref, w32_ref, cos_ref, sin_ref,
                   *refs):
    cast_in, refs = refs[:n_casts], refs[n_casts:]
    qsb_ref, ksb_ref, vsb_ref, or_ref = refs[:4]
    cast_out = refs[4:4 + n_casts]
    w_ref, decay_ref, qdec_ref, kdec_ref, state_ref = refs[4 + n_casts:]

    @pl.when(pl.program_id(0) == 0)
    def _():
        w_ref[...] = w32_ref[...].astype(BF16)
        _retention_tables(lg_ref, decay_ref, qdec_ref, kdec_ref)

    @pl.when(pl.program_id(0) % pos_blocks == 0)
    def _():
        state_ref[...] = jnp.zeros(state_ref.shape, F32)

    _cast_slabs(cast_in, cast_out)
    for r in range(MIX_IN_TM // ROW_GROUP):
        rows = slice(r * ROW_GROUP, (r + 1) * ROW_GROUP)
        h = _rms(x_ref[rows, :], g_ref[...]).astype(BF16)

        def mm(lo, width):
            return jnp.dot(h, w_ref[:, lo:lo + width], preferred_element_type=F32)

        def rotary(t):
            cosf = cos_ref[rows, :]
            sinf = sin_ref[rows, :]
            parts = []
            for hd in range(RET_HEADS):
                th = t[:, hd * LANES:(hd + 1) * LANES]
                parts.append(th * cosf + pltpu.roll(th, LANES // 2, axis=1) * sinf)
            return jnp.concatenate(parts, axis=1)

        c = 3 * SB_WIDTH
        q_r = rotary(mm(c, RET_WIDTH)).astype(BF16); c += RET_WIDTH
        k_r = (rotary(mm(c, RET_WIDTH)) * (RET_KEY_DIM ** -0.5)).astype(BF16); c += RET_WIDTH
        v_r = mm(c, RET_WIDTH).astype(BF16); c += RET_WIDTH
        g_r = mm(c, RET_WIDTH)
        or_ref[rows, :] = _retention_chunk(q_r, k_r, v_r, g_r, lg_ref, decay_ref,
                                           qdec_ref, kdec_ref, state_ref)
        c = 0
        qsb_ref[rows, :] = (mm(c, SB_WIDTH) * SB_Q_SCALE).astype(BF16); c += SB_WIDTH
        ksb_ref[rows, :] = mm(c, SB_WIDTH).astype(BF16); c += SB_WIDTH
        vsb_ref[rows, :] = mm(c, SB_WIDTH).astype(BF16)


def _mix_in(x2d, g, w_in, cosf, sinf, log_gamma, seq, to_bf16=()):
    m = x2d.shape[0]
    steps = m // MIX_IN_TM
    pos_blocks = seq // MIX_IN_TM
    c = RET_CHUNK

    def tile(width):
        return pl.BlockSpec((MIX_IN_TM, width), lambda i: (i, 0))

    rope = pl.BlockSpec((MIX_IN_TM, LANES), lambda i: (i % pos_blocks, 0))
    widths = [SB_WIDTH] * 3 + [RET_WIDTH]
    n_cols = 3 * SB_WIDTH + 4 * RET_WIDTH
    slabs = _slab_specs(to_bf16, steps)
    outs = pl.pallas_call(
        functools.partial(_mix_in_kernel, len(to_bf16), pos_blocks),
        grid=(steps,),
        in_specs=[pl.BlockSpec(memory_space=pltpu.SMEM), tile(D_MODEL),
                  pl.BlockSpec((1, D_MODEL), lambda i: (0, 0)),
                  _resident((D_MODEL, n_cols)), rope, rope] + slabs,
        out_specs=[tile(w) for w in widths] + slabs,
        out_shape=[jax.ShapeDtypeStruct((m, w), BF16) for w in widths]
                  + [jax.ShapeDtypeStruct(w.shape, BF16) for w in to_bf16],
        scratch_shapes=[pltpu.VMEM((D_MODEL, n_cols), BF16),
                        pltpu.VMEM((RET_HEADS, c, c), F32),
                        pltpu.VMEM((RET_HEADS, c, LANES), F32),
                        pltpu.VMEM((RET_HEADS, c, LANES), F32),
                        pltpu.VMEM((RET_HEADS, RET_KEY_DIM, LANES), F32)],
        compiler_params=pltpu.CompilerParams(
            dimension_semantics=("arbitrary",), vmem_limit_bytes=VMEM_LIMIT),
        name="mix_in",
    )(log_gamma, x2d, g, w_in, cosf, sinf, *to_bf16)
    return outs[:len(widths)], outs[len(widths):]


def _sb_kernel(dq_ref, dk_ref, eq_ref, ek_ref, q_ref, k_ref, v_ref, tri_ref, o_ref,
               qh_ref, kh_ref, vh_ref, t_ref, lsig_ref, keep_ref, tot_ref, w_ref,
               later_ref, acc_ref):
    seq = q_ref.shape[1]
    pairs = q_ref.shape[2] // LANES
    seq_tiles = seq // SB_TQ
    n_tiles = pairs * seq_tiles
    lane = lax.broadcasted_iota(jnp.int32, (SB_TQ, LANES), 1)
    head0 = lane < SB_HEAD_DIM
    row = lax.broadcasted_iota(jnp.int32, (SB_TQ, SB_TK), 0)
    col = lax.broadcasted_iota(jnp.int32, (SB_TQ, SB_TK), 1)
    rel = [col - row + c * SB_TK for c in range(SB_NB)]
    tri = tri_ref[...]

    def tile_rows(tile):
        return pl.ds(pl.multiple_of(tile * SB_TQ, SB_TQ), SB_TQ)

    for tile in range(n_tiles):
        src = (0, slice((tile % seq_tiles) * SB_TQ, (tile % seq_tiles + 1) * SB_TQ),
               slice((tile // seq_tiles) * LANES, (tile // seq_tiles + 1) * LANES))
        dst = slice(tile * SB_TQ, (tile + 1) * SB_TQ)
        q = q_ref[src]
        v = v_ref[src]
        zero = jnp.zeros_like(q)
        qh_ref[0, dst, :] = jnp.where(head0, q, zero)
        qh_ref[1, dst, :] = jnp.where(head0, zero, q)
        kh_ref[dst, :] = k_ref[src]
        vh_ref[tile, :SB_SPAN, :] = jnp.where(head0, v, zero)
        vh_ref[tile, SB_SPAN:, :] = jnp.where(head0, zero, v)
    w_ref[1] = jnp.zeros(w_ref.shape[1:], w_ref.dtype)

    def scores(hd, qt, kt):
        return lax.dot_general(qh_ref[hd, tile_rows(qt), :], kh_ref[tile_rows(kt), :],
                               (((1,), (1,)), ((), ())), preferred_element_type=F32)

    def suffix_sums(t, hd, slot, diagonal):
        sp = jnp.maximum(t, 0.0) + jnp.log2(1.0 + jnp.exp2(-jnp.abs(t)))
        lsig_ref[slot, hd] = t - sp
        if diagonal:
            sp = jnp.concatenate(
                [jnp.where(rel[c] < 0, sp[:, c * SB_TK:(c + 1) * SB_TK], 0.0)
                 for c in range(SB_NB)], axis=1)
        return sp[:, 0:1], jnp.dot(sp.astype(BF16), tri, preferred_element_type=F32)

    def stage_sums(parts, hd, slot):
        sp_first, sums = parts
        keep_ref[slot, hd] = sums
        tot_ref[slot, hd] = jnp.broadcast_to(sums[:, 0:1] - sp_first, (SB_TQ, SB_TK))

    def weights(hd, slot, qt, diagonal):
        run = None if diagonal else later_ref[qt, hd]
        for c in range(SB_NB):
            cols = slice(c * SB_TK, (c + 1) * SB_TK)
            arg = lsig_ref[slot, hd, :, cols] + keep_ref[slot, hd, :, cols]
            w = jnp.exp2(arg if run is None else arg + run)
            if diagonal:
                w = jnp.where(rel[c] < 0, w, 0.0)
            lo = hd * SB_SPAN + c * SB_TK
            w_ref[slot, :, lo:lo + SB_TK] = w.astype(BF16)
        tot = tot_ref[slot, hd]
        later_ref[qt, hd] = tot if run is None else run + tot

    def sweep(sq_ref, sk_ref, n_steps, diagonal):
        def step(s):
            return sq_ref[s + 1], sk_ref[s + 1]

        def weighted_values(s, slot):
            qt, kt = step(s)
            pv = jnp.dot(w_ref[slot], vh_ref[kt], preferred_element_type=F32)
            return qt, pv

        def accumulate(qt, pv):
            if diagonal:
                acc_ref[qt] = pv
            else:
                acc_ref[qt] += pv

        def half(s, slot):
            t = [t_ref[1 - slot, hd] for hd in range(2)]
            t_new = [scores(0, *step(s + 2))]
            sums = [suffix_sums(t[0], 0, 1 - slot, diagonal)]
            qt_prev, pv = weighted_values(s - 1, 1 - slot)
            t_new.append(scores(1, *step(s + 2)))
            sums.append(suffix_sums(t[1], 1, 1 - slot, diagonal))
            accumulate(qt_prev, pv)
            qt, _ = step(s)
            for hd in range(2):
                weights(hd, slot, qt, diagonal)
            for hd in range(2):
                t_ref[slot, hd] = t_new[hd]
            for hd in range(2):
                stage_sums(sums[hd], hd, 1 - slot)

        for hd in range(2):
            stage_sums(suffix_sums(scores(hd, *step(0)), hd, 0, diagonal), hd, 0)
        for hd in range(2):
            t_ref[1, hd] = scores(hd, *step(1))

        unroll = _sb_unroll(n_steps)

        def body(i, carry):
            for u in range(unroll):
                half(unroll * i + u, u % 2)
            return carry

        lax.fori_loop(0, n_steps // unroll, body, 0)
        accumulate(*weighted_values(n_steps - 1, 1))

    sweep(dq_ref, dk_ref, n_tiles, True)
    sweep(eq_ref, ek_ref, pairs * seq_tiles * (seq_tiles - 1) // 2, False)
    for tile in range(n_tiles):
        o_ref[0, (tile % seq_tiles) * SB_TQ:(tile % seq_tiles + 1) * SB_TQ,
              (tile // seq_tiles) * LANES:(tile // seq_tiles + 1) * LANES] = (
                  acc_ref[tile].astype(o_ref.dtype))


def _sb_unroll(n_steps):
    return max(u for u in range(2, SB_MAX_UNROLL + 1, 2) if n_steps % u == 0)


def _sb_steps(pairs, seq_tiles):
    n_tiles = pairs * seq_tiles
    diag = [(g, g) for g in range(n_tiles)]
    early = [(p * seq_tiles + t, p * seq_tiles + kt) for p in range(pairs)
             for t in range(1, seq_tiles) for kt in range(t - 1, -1, -1)]
    tables = []
    for steps in (diag, early):
        steps = [(n_tiles, 0)] + steps + [steps[-1]] * 2
        tables += [jnp.asarray([s[0] for s in steps], jnp.int32),
                   jnp.asarray([s[1] for s in steps], jnp.int32)]
    return tables


def _sb_attention(q, k, v, tri):
    b, seq, width = q.shape
    pairs = width // LANES
    seq_tiles = seq // SB_TQ
    n_tiles = pairs * seq_tiles
    spec = pl.BlockSpec((1, seq, width), lambda bi: (bi, 0, 0))
    smem = pl.BlockSpec(memory_space=pltpu.SMEM)
    return pl.pallas_call(
        _sb_kernel,
        grid=(b,),
        in_specs=[smem, smem, smem, smem, spec, spec, spec,
                  pl.BlockSpec((SB_SPAN, SB_SPAN), lambda bi: (0, 0))],
        out_specs=spec,
        out_shape=jax.ShapeDtypeStruct((b, seq, width), BF16),
        scratch_shapes=[pltpu.VMEM((2, n_tiles * SB_TQ, LANES), BF16),
                        pltpu.VMEM((n_tiles * SB_TQ, LANES), BF16),
                        pltpu.VMEM((n_tiles, 2 * SB_SPAN, LANES), BF16),
                        pltpu.VMEM((2, 2, SB_TQ, SB_SPAN), F32),
                        pltpu.VMEM((2, 2, SB_TQ, SB_SPAN), F32),
                        pltpu.VMEM((2, 2, SB_TQ, SB_SPAN), F32),
                        pltpu.VMEM((2, 2, SB_TQ, SB_TK), F32),
                        pltpu.VMEM((2, SB_TQ, 2 * SB_SPAN), BF16),
                        pltpu.VMEM((n_tiles, 2, SB_TQ, SB_TK), F32),
                        pltpu.VMEM((n_tiles + 1, SB_TQ, LANES), F32)],
        compiler_params=pltpu.CompilerParams(
            dimension_semantics=("arbitrary",), vmem_limit_bytes=SB_VMEM_LIMIT),
        name="sb_attn",
    )(*_sb_steps(pairs, seq_tiles), q, k, v, tri)


def _retention_tables(lg_ref, decay_ref, qdec_ref, kdec_ref):
    c = RET_CHUNK
    diff = (lax.broadcasted_iota(jnp.int32, (c, c), 0)
            - lax.broadcasted_iota(jnp.int32, (c, c), 1)).astype(F32)
    idx = lax.broadcasted_iota(jnp.int32, (c, LANES), 0).astype(F32)
    for h in range(RET_HEADS):
        lg = lg_ref[h]
        decay_ref[h] = jnp.where(diff >= 0, jnp.exp(jnp.maximum(diff, 0.0) * lg), 0.0)
        qdec_ref[h] = jnp.exp((idx + 1.0) * lg)
        kdec_ref[h] = jnp.exp((c - 1.0 - idx) * lg)


def _retention_chunk(q, k, v, gate, lg_ref, decay_ref, qdec_ref, kdec_ref, state_ref):
    c = RET_CHUNK
    heads = range(RET_HEADS)
    q, k, v = ([a[:, h * LANES:(h + 1) * LANES] for h in heads] for a in (q, k, v))
    state = [state_ref[h] for h in heads]
    scores = [lax.dot_general(q[h], k[h], (((1,), (1,)), ((), ())),
                              preferred_element_type=F32) for h in heads]
    cross = [jnp.dot((q[h].astype(F32) * qdec_ref[h]).astype(BF16),
                     state[h].astype(BF16), preferred_element_type=F32)
             for h in heads]
    kd_t = [(k[h].astype(F32) * kdec_ref[h]).T.astype(BF16) for h in heads]
    kv = [jnp.dot(kd_t[h], v[h], preferred_element_type=F32) for h in heads]
    inner = [jnp.dot((scores[h] * decay_ref[h]).astype(BF16), v[h],
                     preferred_element_type=F32) for h in heads]
    for h in heads:
        chunk_decay = jnp.exp(jnp.full((1, LANES), float(c), F32) * lg_ref[h])
        state_ref[h] = chunk_decay * state[h] + kv[h]
    out = []
    for h in heads:
        o = inner[h] + cross[h]
        mu = jnp.mean(o, axis=-1, keepdims=True)
        ctr = o - mu
        var = jnp.mean(ctr * ctr, axis=-1, keepdims=True)
        g = gate[:, h * LANES:(h + 1) * LANES]
        out.append((ctr * lax.rsqrt(var + GN_EPS) * (g * jax.nn.sigmoid(g))).astype(BF16))
    return jnp.concatenate(out, axis=1)


def _merge_kernel(n_gate_blocks, x_ref, osb_ref, or_ref, gpre_ref, *refs):
    gate32_refs, refs = refs[:n_gate_blocks], refs[n_gate_blocks:]
    (wsb32_ref, wret32_ref, wout32_ref, gpost_ref, o_ref,
     wgate_ref, wsb_ref, wret_ref, wout_ref) = refs

    @pl.when(pl.program_id(0) == 0)
    def _():
        width = gate32_refs[0].shape[1]
        for b, ref in enumerate(gate32_refs):
            wgate_ref[:, b * width:(b + 1) * width] = ref[...].astype(BF16)
        wsb_ref[...] = wsb32_ref[...].astype(BF16)
        wret_ref[...] = wret32_ref[...].astype(BF16)
        wout_ref[...] = wout32_ref[...].astype(BF16)

    for r in range(MIX_TM // ROW_GROUP):
        rows = slice(r * ROW_GROUP, (r + 1) * ROW_GROUP)
        x = x_ref[rows, :]
        p_sb = jnp.dot(osb_ref[rows, :], wsb_ref[...], preferred_element_type=F32)
        p_ret = jnp.dot(or_ref[rows, :], wret_ref[...], preferred_element_type=F32)
        h = _rms(x, gpre_ref[...]).astype(BF16)
        gate_sb = jnp.dot(h, wgate_ref[:, :D_MODEL], preferred_element_type=F32)
        merged = jax.nn.sigmoid(gate_sb) * p_sb
        gate_ret = jnp.dot(h, wgate_ref[:, D_MODEL:], preferred_element_type=F32)
        merged = merged + jax.nn.sigmoid(gate_ret) * p_ret
        y = jnp.dot(merged.astype(BF16), wout_ref[...], preferred_element_type=F32)
        o_ref[rows, :] = x + _rms(y, gpost_ref[...])


def _merge(x2d, o_sb, o_r, g_pre, w_in, w_sb, w_ret, w_out, g_post):
    m = x2d.shape[0]
    gate_lo = w_in.shape[1] - 2 * D_MODEL
    gate_block = math.gcd(gate_lo, 2 * D_MODEL)
    n_gate_blocks = 2 * D_MODEL // gate_block

    def tile(width):
        return pl.BlockSpec((MIX_TM, width), lambda i: (i, 0))

    def gate_cols(b):
        return pl.BlockSpec((D_MODEL, gate_block),
                            lambda i: (0, gate_lo // gate_block + b),
                            pipeline_mode=pl.Buffered(1))

    gain = pl.BlockSpec((1, D_MODEL), lambda i: (0, 0))
    return pl.pallas_call(
        functools.partial(_merge_kernel, n_gate_blocks),
        grid=(m // MIX_TM,),
        in_specs=[tile(D_MODEL), tile(SB_WIDTH), tile(RET_WIDTH), gain]
                 + [gate_cols(b) for b in range(n_gate_blocks)]
                 + [_resident((SB_WIDTH, D_MODEL)), _resident((RET_WIDTH, D_MODEL)),
                    _resident((D_MODEL, D_MODEL)), gain],
        out_specs=tile(D_MODEL),
        out_shape=jax.ShapeDtypeStruct((m, D_MODEL), F32),
        scratch_shapes=[pltpu.VMEM((D_MODEL, 2 * D_MODEL), BF16),
                        pltpu.VMEM((SB_WIDTH, D_MODEL), BF16),
                        pltpu.VMEM((RET_WIDTH, D_MODEL), BF16),
                        pltpu.VMEM((D_MODEL, D_MODEL), BF16)],
        compiler_params=pltpu.CompilerParams(
            dimension_semantics=("arbitrary",), vmem_limit_bytes=MERGE_VMEM_LIMIT),
        name="merge",
    )(x2d, o_sb, o_r, g_pre, *([w_in] * n_gate_blocks), w_sb, w_ret, w_out, g_post)


def _rope_tables(seq):
    half = RET_KEY_DIM // 2
    pos = np.arange(seq, dtype=np.float64)
    inv_freq = ROPE_BASE ** (-np.arange(half, dtype=np.float64) / half)
    ang = pos[:, None] * inv_freq[None, :]
    cos, sin = np.cos(ang), np.sin(ang)
    return (jnp.asarray(np.concatenate([cos, cos], axis=-1), dtype=F32),
            jnp.asarray(np.concatenate([-sin, sin], axis=-1), dtype=F32))


def _suffix_sum_matrix():
    j = np.arange(SB_SPAN)[:, None]
    s = np.arange(SB_SPAN)[None, :]
    return jnp.asarray(-(j > s).astype(np.float32), dtype=BF16)


def kernel(x, g_ffn1_pre, g_ffn1_post, w_ffn1_gate, w_ffn1_up, w_ffn1_down,
           g_mix_pre, w_in, w_proj_sb, w_proj_ret, w_out, g_mix_post,
           g_ffn2_pre, g_ffn2_post, w_ffn2_gate, w_ffn2_up, w_ffn2_down):
    b, seq, d = x.shape
    depth = w_in.shape[0]
    log_gamma = jnp.log(1.0 - 2.0 ** (-5.0 - jnp.arange(RET_HEADS, dtype=F32)))
    cosf, sinf = _rope_tables(seq)
    tri = _suffix_sum_matrix()
    x2d = x.reshape(b * seq, d)
    for l in range(depth):
        x2d = _ffn(x2d, g_ffn1_pre[l][None], g_ffn1_post[l][None],
                   w_ffn1_gate[l].astype(BF16), w_ffn1_up[l].astype(BF16),
                   w_ffn1_down[l].astype(BF16))
        (q_sb, k_sb, v_sb, o_r), (w2_gate, w2_up, w2_down) = _mix_in(
            x2d, g_mix_pre[l][None], w_in[l], cosf, sinf, log_gamma, seq,
            to_bf16=(w_ffn2_gate[l], w_ffn2_up[l], w_ffn2_down[l]))
        shp = (b, seq, SB_WIDTH)
        o_sb = _sb_attention(q_sb.reshape(shp), k_sb.reshape(shp),
                             v_sb.reshape(shp), tri)
        x2d = _merge(x2d, o_sb.reshape(b * seq, SB_WIDTH), o_r,
                     g_mix_pre[l][None], w_in[l], w_proj_sb[l], w_proj_ret[l],
                     w_out[l], g_mix_post[l][None])
        x2d = _ffn(x2d, g_ffn2_pre[l][None], g_ffn2_post[l][None],
                   w2_gate, w2_up, w2_down)
    return x2d.reshape(b, seq, d)
```
